```python
import jax, jax.numpy as jnp
from jax import lax
import numpy as np

D_MODEL = 1024
BATCH = 2
SEQ = 8192
DEPTH = 2

CONV_CHANNELS = 512
CONV_GROUPS = 8
CONV_WIDTH = 3
N_Q_HEADS = 8
N_KV_HEADS = 2
HEAD_DIM = 64
ATTN_WIDTH = N_Q_HEADS * HEAD_DIM
WINDOW = 128
BLOCK = 128
MIX_WIDTH = CONV_CHANNELS + ATTN_WIDTH
IN_COLS = 3 * CONV_CHANNELS + (N_Q_HEADS + 2 * N_KV_HEADS) * HEAD_DIM
D_FF = -((-8 * D_MODEL) // (3 * 256)) * 256
EPS = 1e-6
NEG_INF = -1e30

kernel_name = "hymba_style_conv_swa_sink_hybrid"


def rms_norm(x, g):
    xf = x.astype(jnp.float32)
    y = xf * lax.rsqrt(jnp.mean(xf * xf, axis=-1, keepdims=True) + EPS)
    return (y * g.astype(jnp.float32)).astype(x.dtype)


def short_gated_conv(b_gate, c_gate, h, conv_w):
    seq = h.shape[1]
    u = c_gate * h
    up = jnp.pad(u, ((0, 0), (CONV_WIDTH - 1, 0), (0, 0)))
    y = conv_w[0] * up[:, 0:seq]
    for tap in range(1, CONV_WIDTH):
        y = y + conv_w[tap] * up[:, tap:tap + seq]
    return b_gate * y


def band_keys(t, nb):
    b = t.shape[0]
    tb = t.reshape(b, nb, BLOCK, t.shape[2], t.shape[3])
    prev = jnp.pad(tb[:, :-1], ((0, 0), (1, 0), (0, 0), (0, 0), (0, 0)))
    return jnp.concatenate([prev, tb], axis=2)


def sliding_window_attention_with_sinks(q, k, v, sinks):
    b, seq = q.shape[0], q.shape[1]
    nb = seq // BLOCK
    grp = N_Q_HEADS // N_KV_HEADS
    qb = q.reshape(b, nb, BLOCK, N_KV_HEADS, grp, HEAD_DIM)
    kb = band_keys(k, nb)
    vb = band_keys(v, nb)
    scale = HEAD_DIM ** -0.5
    s = jnp.einsum('bnqhgd,bnkhd->bnhgqk', qb, kb).astype(jnp.float32) * scale
    qpos = jnp.arange(nb)[:, None] * BLOCK + jnp.arange(BLOCK)[None, :]
    kpos = (jnp.arange(nb)[:, None] - 1) * BLOCK + jnp.arange(2 * BLOCK)[None, :]
    diff = qpos[:, :, None] - kpos[:, None, :]
    valid = (diff >= 0) & (diff < WINDOW) & (kpos[:, None, :] >= 0)
    s = jnp.where(valid[None, :, None, None], s, NEG_INF)
    sink = sinks.astype(jnp.float32).reshape(N_KV_HEADS, grp)[None, None, :, :, None, None]
    m = jnp.maximum(jnp.max(s, axis=-1, keepdims=True), sink)
    p = jnp.exp(s - m)
    denom = jnp.sum(p, axis=-1, keepdims=True) + jnp.exp(sink - m)
    probs = (p / denom).astype(v.dtype)
    o = jnp.einsum('bnhgqk,bnkhd->bnqhgd', probs, vb)
    return o.reshape(b, seq, ATTN_WIDTH)


def setup_inputs(seed: int = 0) -> dict:
    key = jax.random.key(seed)
    ks = jax.random.split(key, 16)
    f32 = jnp.float32

    def gain(k, shape):
        return 1.0 + 0.02 * jax.random.normal(k, shape, f32)

    return {
        "x": jax.random.normal(ks[0], (BATCH, SEQ, D_MODEL), f32),
        "norm1_g": gain(ks[1], (DEPTH, D_MODEL)),
        "w_in": jax.random.normal(ks[2], (DEPTH, D_MODEL, IN_COLS), f32) * D_MODEL ** -0.5,
        "conv_w": jax.random.normal(ks[3], (DEPTH, CONV_WIDTH, CONV_CHANNELS), f32) * CONV_WIDTH ** -0.5,
        "q_norm_g": gain(ks[4], (DEPTH, HEAD_DIM)),
        "k_norm_g": gain(ks[5], (DEPTH, HEAD_DIM)),
        "sinks": 0.5 * jax.random.normal(ks[6], (DEPTH, N_Q_HEADS), f32),
        "conv_out_g": gain(ks[7], (DEPTH, CONV_CHANNELS)),
        "attn_out_g": gain(ks[8], (DEPTH, ATTN_WIDTH)),
        "w_o": jax.random.normal(ks[9], (DEPTH, MIX_WIDTH, D_MODEL), f32) * MIX_WIDTH ** -0.5,
        "norm2_g": gain(ks[10], (DEPTH, D_MODEL)),
        "w_gate": jax.random.normal(ks[11], (DEPTH, D_MODEL, D_FF), f32) * D_MODEL ** -0.5,
        "w_up": jax.random.normal(ks[12], (DEPTH, D_MODEL, D_FF), f32) * D_MODEL ** -0.5,
        "w_down": jax.random.normal(ks[13], (DEPTH, D_FF, D_MODEL), f32) * D_FF ** -0.5,
    }


def reference(x, norm1_g, w_in, conv_w, q_norm_g, k_norm_g, sinks, conv_out_g,
              attn_out_g, w_o, norm2_g, w_gate, w_up, w_down):
    b, seq = x.shape[0], x.shape[1]
    c = CONV_CHANNELS
    o_q = 3 * c
    o_k = o_q + ATTN_WIDTH
    o_v = o_k + N_KV_HEADS * HEAD_DIM
    for l in range(DEPTH):
        h = rms_norm(x, norm1_g[l])
        proj = h @ w_in[l]
        b_gate = proj[..., 0:c]
        c_gate = proj[..., c:2 * c]
        hc = proj[..., 2 * c:3 * c]
        q = proj[..., o_q:o_k].reshape(b, seq, N_Q_HEADS, HEAD_DIM)
        k = proj[..., o_k:o_v].reshape(b, seq, N_KV_HEADS, HEAD_DIM)
        v = proj[..., o_v:].reshape(b, seq, N_KV_HEADS, HEAD_DIM)

        conv_out = short_gated_conv(b_gate, c_gate, hc, conv_w[l])

        q = rms_norm(q, q_norm_g[l])
        k = rms_norm(k, k_norm_g[l])
        attn_out = sliding_window_attention_with_sinks(q, k, v, sinks[l])

        mix = jnp.concatenate([rms_norm(conv_out, conv_out_g[l]),
                               rms_norm(attn_out, attn_out_g[l])], axis=-1)
        x = x + mix @ w_o[l]

        h2 = rms_norm(x, norm2_g[l])
        x = x + (jax.nn.silu(h2 @ w_gate[l]) * (h2 @ w_up[l])) @ w_down[l]
    return x
```

```python
import functools

import jax
import jax.numpy as jnp
from jax import lax
from jax.experimental import pallas as pl
from jax.experimental.pallas import tpu as pltpu

D_MODEL = 1024
CONV_CHANNELS = 512
CONV_WIDTH = 3
N_Q_HEADS = 8
N_KV_HEADS = 2
HEAD_DIM = 64
ATTN_WIDTH = N_Q_HEADS * HEAD_DIM
BLOCK = 128
IN_COLS = 3 * CONV_CHANNELS + (N_Q_HEADS + 2 * N_KV_HEADS) * HEAD_DIM
D_FF = 2816
EPS = 1e-6
NEG_INF = -1e30

LANES = 128
SUBLANES = 8
Q_COLS = ATTN_WIDTH // LANES
KV_OFF = 3 * CONV_CHANNELS
VMEM_LIMIT_BYTES = 56 * 1024 * 1024

TM_MIXER = 512
TM_FFN = 512

F32 = jnp.float32
BF16 = jnp.bfloat16


def _rms(x, g):
    ms = jnp.mean(x * x, axis=-1, keepdims=True)
    return x * lax.rsqrt(ms + EPS) * g


def _mixer_kernel(sinks_ref, x_ref, g1_ref, win_ref, convw_ref, qg_ref, kg_ref, cog_ref, aog_ref, wo_ref,
                  out_ref, ubuf, qbuf, kk, vv, abuf, *, tm):
    j = pl.program_id(1)
    c3 = CONV_CHANNELS

    @pl.when(j == 0)
    def _():
        ubuf[0:SUBLANES, :] = jnp.zeros((SUBLANES, c3), F32)
        kk[:, 0:BLOCK, :] = jnp.zeros((4, BLOCK, LANES), BF16)
        vv[:, 0:BLOCK, :] = jnp.zeros((4, BLOCK, LANES), BF16)

    x = x_ref[0]
    h = _rms(x, g1_ref[...]).astype(BF16)

    pc = jnp.dot(h, win_ref[:, 0:3 * c3], preferred_element_type=F32)
    b_gate = pc[:, 0:c3]
    u = pc[:, c3:2 * c3] * pc[:, 2 * c3:3 * c3]
    ubuf[SUBLANES:SUBLANES + tm, :] = u
    y = (convw_ref[0:1, :] * ubuf[SUBLANES - 2:SUBLANES - 2 + tm, :]
         + convw_ref[1:2, :] * ubuf[SUBLANES - 1:SUBLANES - 1 + tm, :]
         + convw_ref[2:3, :] * u)
    ubuf[0:SUBLANES, :] = ubuf[tm:tm + SUBLANES, :]
    mix_c = _rms(b_gate * y, cog_ref[...]).astype(BF16)

    qkv = jnp.dot(h, win_ref[:, KV_OFF:IN_COLS], preferred_element_type=F32)
    lane = lax.broadcasted_iota(jnp.int32, (1, LANES), 1)
    lo_half = lane < HEAD_DIM

    def head_norm(t, g2):
        sq = t * t
        s_lo = jnp.sum(jnp.where(lo_half, sq, 0.0), axis=-1, keepdims=True)
        s_hi = jnp.sum(jnp.where(lo_half, 0.0, sq), axis=-1, keepdims=True)
        ms = jnp.where(lo_half, s_lo, s_hi) * (1.0 / HEAD_DIM)
        return t * lax.rsqrt(ms + EPS) * g2

    for c in range(Q_COLS):
        qbuf[:, c * LANES:(c + 1) * LANES] = head_norm(qkv[:, c * LANES:(c + 1) * LANES], qg_ref[...]).astype(BF16)

    def place(t, dst):
        a0 = jnp.where(lo_half, t, 0.0)
        b1 = jnp.where(lo_half, 0.0, t)
        dst[0, BLOCK:BLOCK + tm, :] = a0.astype(BF16)
        dst[1, BLOCK:BLOCK + tm, :] = pltpu.roll(a0, HEAD_DIM, 1).astype(BF16)
        dst[2, BLOCK:BLOCK + tm, :] = pltpu.roll(b1, HEAD_DIM, 1).astype(BF16)
        dst[3, BLOCK:BLOCK + tm, :] = b1.astype(BF16)

    place(head_norm(qkv[:, ATTN_WIDTH:ATTN_WIDTH + LANES], kg_ref[...]), kk)
    place(qkv[:, ATTN_WIDTH + LANES:ATTN_WIDTH + 2 * LANES], vv)

    row = lax.broadcasted_iota(jnp.int32, (BLOCK, 2 * BLOCK), 0)
    col = lax.broadcasted_iota(jnp.int32, (BLOCK, 2 * BLOCK), 1)
    band = (col > row) & (col <= row + BLOCK)

    def attend(b, carry):
        r0 = pl.multiple_of(b * BLOCK, BLOCK)
        min_col = jnp.where((j == 0) & (b == 0), BLOCK, 0)
        mask = band & (col >= min_col)
        for c in range(Q_COLS):
            g = c // (Q_COLS // N_KV_HEADS)
            qc = qbuf[pl.ds(r0, BLOCK), c * LANES:(c + 1) * LANES]
            acc = None
            inv = []
            for half in range(2):
                sink = sinks_ref[2 * c + half]
                kt = kk[2 * g + half, pl.ds(r0, 2 * BLOCK), :]
                s = lax.dot_general(qc, kt, (((1,), (1,)), ((), ())), preferred_element_type=F32)
                s = jnp.where(mask, s, NEG_INF)
                m = jnp.maximum(jnp.max(s, axis=-1, keepdims=True), sink)
                p = jnp.exp(s - m)
                inv.append(1.0 / (jnp.sum(p, axis=-1, keepdims=True) + jnp.exp(sink - m)))
                vt = vv[2 * g + half, pl.ds(r0, 2 * BLOCK), :]
                o = jnp.dot(p.astype(BF16), vt, preferred_element_type=F32)
                acc = o if acc is None else acc + o
            abuf[pl.ds(r0, BLOCK), c * LANES:(c + 1) * LANES] = acc * jnp.where(lo_half, inv[0], inv[1])
        return carry

    lax.fori_loop(0, tm // BLOCK, attend, 0)

    kk[:, 0:BLOCK, :] = kk[:, tm:tm + BLOCK, :]
    vv[:, 0:BLOCK, :] = vv[:, tm:tm + BLOCK, :]

    mix_a = _rms(abuf[...], aog_ref[...]).astype(BF16)
    out_ref[0] = (x
                  + jnp.dot(mix_c, wo_ref[0:c3, :], preferred_element_type=F32)
                  + jnp.dot(mix_a, wo_ref[c3:c3 + ATTN_WIDTH, :], preferred_element_type=F32))


def _ffn_kernel(x_ref, g_ref, wg_ref, wu_ref, wd_ref, out_ref):
    x = x_ref[...]
    h = _rms(x, g_ref[...]).astype(BF16)
    gate = jnp.dot(h, wg_ref[...], preferred_element_type=F32)
    up = jnp.dot(h, wu_ref[...], preferred_element_type=F32)
    act = (gate * jax.nn.sigmoid(gate) * up).astype(BF16)
    out_ref[...] = x + jnp.dot(act, wd_ref[...], preferred_element_type=F32)


def _resident(shape, index_map):
    return pl.BlockSpec(shape, index_map, pipeline_mode=pl.Buffered(1))


def _mixer(x, sinks, g1, w_in, conv_w, qg2, kg2, cog, aog, w_o, layer, tm):
    b, seq, d = x.shape
    lsel3 = lambda bi, ji: (layer, 0, 0)
    vec = lambda n: _resident((None, 1, n), lsel3)
    return pl.pallas_call(
        functools.partial(_mixer_kernel, tm=tm),
        name=f"mixer_l{layer}",
        grid=(b, seq // tm),
        in_specs=[
            pl.BlockSpec(memory_space=pltpu.SMEM),
            pl.BlockSpec((1, tm, d), lambda bi, ji: (bi, ji, 0)),
            vec(d),
            _resident((None, d, IN_COLS), lsel3),
            _resident((None, SUBLANES, CONV_CHANNELS), lsel3),
            vec(LANES),
            vec(LANES),
            vec(CONV_CHANNELS),
            vec(ATTN_WIDTH),
            _resident((None, CONV_CHANNELS + ATTN_WIDTH, d), lsel3),
        ],
        out_specs=pl.BlockSpec((1, tm, d), lambda bi, ji: (bi, ji, 0)),
        out_shape=jax.ShapeDtypeStruct(x.shape, x.dtype),
        scratch_shapes=[
            pltpu.VMEM((tm + SUBLANES, CONV_CHANNELS), F32),
            pltpu.VMEM((tm, ATTN_WIDTH), BF16),
            pltpu.VMEM((4, tm + BLOCK, LANES), BF16),
            pltpu.VMEM((4, tm + BLOCK, LANES), BF16),
            pltpu.VMEM((tm, ATTN_WIDTH), F32),
        ],
        compiler_params=pltpu.CompilerParams(
            dimension_semantics=("arbitrary", "arbitrary"),
            vmem_limit_bytes=VMEM_LIMIT_BYTES,
        ),
    )(sinks, x, g1, w_in, conv_w, qg2, kg2, cog, aog, w_o)


def _ffn(x2d, g, w_gate, w_up, w_down, layer, tm):
    n, d = x2d.shape
    lsel3 = lambda i: (layer, 0, 0)
    return pl.pallas_call(
        _ffn_kernel,
        name=f"ffn_l{layer}",
        grid=(n // tm,),
        in_specs=[
            pl.BlockSpec((tm, d), lambda i: (i, 0)),
            _resident((None, 1, d), lsel3),
            _resident((None, d, D_FF), lsel3),
            _resident((None, d, D_FF), lsel3),
            _resident((None, D_FF, d), lsel3),
        ],
        out_specs=pl.BlockSpec((tm, d), lambda i: (i, 0)),
        out_shape=jax.ShapeDtypeStruct(x2d.shape, x2d.dtype),
        compiler_params=pltpu.CompilerParams(
            dimension_semantics=("arbitrary",),
            vmem_limit_bytes=VMEM_LIMIT_BYTES,
        ),
    )(x2d, g, w_gate, w_up, w_down)


def kernel(x, norm1_g, w_in, conv_w, q_norm_g, k_norm_g, sinks, conv_out_g, attn_out_g, w_o, norm2_g,
           w_gate, w_up, w_down):
    b, seq, d = x.shape
    depth = w_in.shape[0]
    assert seq % TM_MIXER == 0 and (b * seq) % TM_FFN == 0 and TM_MIXER % BLOCK == 0

    row3 = lambda a: a.reshape(depth, 1, a.shape[-1])
    qg2 = row3(jnp.tile(q_norm_g * (HEAD_DIM ** -0.5), (1, 2)))
    kg2 = row3(jnp.tile(k_norm_g, (1, 2)))
    conv_w8 = jnp.pad(conv_w, ((0, 0), (0, SUBLANES - CONV_WIDTH), (0, 0)))
    w_in_b, w_o_b = w_in.astype(BF16), w_o.astype(BF16)
    w_gate_b, w_up_b, w_down_b = w_gate.astype(BF16), w_up.astype(BF16), w_down.astype(BF16)
    g1, g2 = row3(norm1_g), row3(norm2_g)
    cog, aog = row3(conv_out_g), row3(attn_out_g)

    for l in range(depth):
        x = _mixer(x, sinks[l], g1, w_in_b, conv_w8, qg2, kg2, cog, aog, w_o_b, l, TM_MIXER)
        x = _ffn(x.reshape(b * seq, d), g2, w_gate_b, w_up_b, w_down_b, l, TM_FFN).reshape(b, seq, d)
    return x
```

```python
import functools

import jax
import jax.numpy as jnp
from jax import lax
from jax.experimental import pallas as pl
from jax.experimental.pallas import tpu as pltpu

D_MODEL = 1024
CONV_CHANNELS = 512
CONV_WIDTH = 3
N_Q_HEADS = 8
N_KV_HEADS = 2
HEAD_DIM = 64
ATTN_WIDTH = N_Q_HEADS * HEAD_DIM
BLOCK = 128
IN_COLS = 3 * CONV_CHANNELS + (N_Q_HEADS + 2 * N_KV_HEADS) * HEAD_DIM
D_FF = 2816
EPS = 1e-6
NEG_INF = -1e30

LANES = 128
SUBLANES = 8
Q_COLS = ATTN_WIDTH // LANES
COLS_PER_KV = Q_COLS // N_KV_HEADS
LOG2E = 1.4426950408889634
KV_OFF = 3 * CONV_CHANNELS
VMEM_LIMIT_BYTES = 56 * 1024 * 1024

TM_MIXER = 512
TM_FFN = 512

F32 = jnp.float32
BF16 = jnp.bfloat16


def _rms(x, g):
    ms = jnp.mean(x * x, axis=-1, keepdims=True)
    return x * lax.rsqrt(ms + EPS) * g


def _mixer_kernel(x_ref, g1_ref, win_ref, convw_ref, qg_ref, kg_ref, sink_ref, cog_ref, aog_ref, wo_ref,
                  out_ref, ubuf, qq, kk, vt, sbuf, pbuf, abuf, cbuf, *, tm):
    j = pl.program_id(1)
    c3 = CONV_CHANNELS
    nblk = tm // BLOCK

    @pl.when(j == 0)
    def _():
        ubuf[0:SUBLANES, :] = jnp.zeros((SUBLANES, c3), F32)
        kk[:, 0:BLOCK, :] = jnp.zeros((4, BLOCK, LANES), BF16)
        vt[:, :, 0:BLOCK] = jnp.zeros((4, LANES, BLOCK), BF16)

    x = x_ref[0]
    h = _rms(x, g1_ref[...]).astype(BF16)

    def conv_chunk(lo, hi):
        b_gate = jnp.dot(h, win_ref[:, lo:hi], preferred_element_type=F32)
        u = (jnp.dot(h, win_ref[:, c3 + lo:c3 + hi], preferred_element_type=F32)
             * jnp.dot(h, win_ref[:, 2 * c3 + lo:2 * c3 + hi], preferred_element_type=F32))
        ubuf[SUBLANES:SUBLANES + tm, lo:hi] = u
        y = (convw_ref[0:1, lo:hi] * ubuf[SUBLANES - 2:SUBLANES - 2 + tm, lo:hi]
             + convw_ref[1:2, lo:hi] * ubuf[SUBLANES - 1:SUBLANES - 1 + tm, lo:hi]
             + convw_ref[2:3, lo:hi] * u)
        ubuf[0:SUBLANES, lo:hi] = ubuf[tm:tm + SUBLANES, lo:hi]
        cbuf[:, lo:hi] = b_gate * y

    qkv = jnp.dot(h, win_ref[:, KV_OFF:IN_COLS], preferred_element_type=F32)
    lane = lax.broadcasted_iota(jnp.int32, (1, LANES), 1)
    lo_half = lane < HEAD_DIM

    def head_norm(t, g2):
        sq = t * t
        s_lo = jnp.sum(jnp.where(lo_half, sq, 0.0), axis=-1, keepdims=True)
        s_hi = jnp.sum(jnp.where(lo_half, 0.0, sq), axis=-1, keepdims=True)
        ms = jnp.where(lo_half, s_lo, s_hi) * (1.0 / HEAD_DIM)
        return t * lax.rsqrt(ms + EPS) * g2

    for c in range(Q_COLS):
        qn = head_norm(qkv[:, c * LANES:(c + 1) * LANES], qg_ref[...]).astype(BF16)
        g, r = divmod(c, COLS_PER_KV)
        for b in range(nblk):
            qq[g, (COLS_PER_KV * b + r) * BLOCK:(COLS_PER_KV * b + r + 1) * BLOCK, :] = qn[b * BLOCK:(b + 1) * BLOCK, :]

    kn = head_norm(qkv[:, ATTN_WIDTH:ATTN_WIDTH + LANES], kg_ref[...])
    k_a0 = jnp.where(lo_half, kn, 0.0)
    k_b1 = jnp.where(lo_half, 0.0, kn)
    kk[0, BLOCK:BLOCK + tm, :] = k_a0.astype(BF16)
    kk[1, BLOCK:BLOCK + tm, :] = pltpu.roll(k_a0, HEAD_DIM, 1).astype(BF16)
    kk[2, BLOCK:BLOCK + tm, :] = pltpu.roll(k_b1, HEAD_DIM, 1).astype(BF16)
    kk[3, BLOCK:BLOCK + tm, :] = k_b1.astype(BF16)

    v_t = qkv[:, ATTN_WIDTH + LANES:ATTN_WIDTH + 2 * LANES].T
    zeros_t = jnp.zeros((HEAD_DIM, tm), F32)
    for g in range(N_KV_HEADS):
        v_g = v_t[g * HEAD_DIM:(g + 1) * HEAD_DIM, :]
        vt[2 * g, :, BLOCK:BLOCK + tm] = jnp.concatenate([v_g, zeros_t], axis=0).astype(BF16)
        vt[2 * g + 1, :, BLOCK:BLOCK + tm] = jnp.concatenate([zeros_t, v_g], axis=0).astype(BF16)

    units = [(b, g, half) for b in range(nblk) for g in range(N_KV_HEADS) for half in range(2)]

    for u, (b, g, half) in enumerate(units):
        keys = kk[2 * g + half, b * BLOCK:(b + 2) * BLOCK, :]
        qs = qq[g, COLS_PER_KV * b * BLOCK:COLS_PER_KV * (b + 1) * BLOCK, :]
        sbuf[u] = lax.dot_general(keys, qs, (((1,), (1,)), ((), ())), preferred_element_type=F32)

    conv_chunk(0, c3 // 2)

    key_i = lax.broadcasted_iota(jnp.int32, (2 * BLOCK, COLS_PER_KV * BLOCK), 0)
    qry_i = lax.broadcasted_iota(jnp.int32, (2 * BLOCK, COLS_PER_KV * BLOCK), 1) & (BLOCK - 1)
    band = (key_i > qry_i) & (key_i <= qry_i + BLOCK)
    first_band = band & (key_i >= jnp.where(j == 0, BLOCK, 0))

    inv = []
    for u, (b, g, half) in enumerate(units):
        s = jnp.where(first_band if b == 0 else band, sbuf[u], NEG_INF)
        sink = sink_ref[2 * g + half:2 * g + half + 1, :]
        m = jnp.maximum(jnp.max(s, axis=0, keepdims=True), sink)
        p = jnp.exp2(s - m)
        inv.append(1.0 / (jnp.sum(p, axis=0, keepdims=True) + jnp.exp2(sink - m)))
        pbuf[u] = p.astype(BF16)

    conv_chunk(c3 // 2, c3)

    for b in range(nblk):
        for g in range(N_KV_HEADS):
            u0 = (b * N_KV_HEADS + g) * 2
            o_t = (jnp.dot(vt[2 * g, :, b * BLOCK:(b + 2) * BLOCK], pbuf[u0], preferred_element_type=F32)
                   + jnp.dot(vt[2 * g + 1, :, b * BLOCK:(b + 2) * BLOCK], pbuf[u0 + 1], preferred_element_type=F32))
            scale = jnp.concatenate([jnp.broadcast_to(inv[u0], (HEAD_DIM, COLS_PER_KV * BLOCK)),
                                     jnp.broadcast_to(inv[u0 + 1], (HEAD_DIM, COLS_PER_KV * BLOCK))], axis=0)
            o_t = o_t * scale
            for r in range(COLS_PER_KV):
                c = COLS_PER_KV * g + r
                abuf[b * BLOCK:(b + 1) * BLOCK, c * LANES:(c + 1) * LANES] = o_t[:, r * BLOCK:(r + 1) * BLOCK].T

    kk[:, 0:BLOCK, :] = kk[:, tm:tm + BLOCK, :]
    vt[:, :, 0:BLOCK] = vt[:, :, tm:tm + BLOCK]

    mix_c = _rms(cbuf[...], cog_ref[...]).astype(BF16)
    mix_a = _rms(abuf[...], aog_ref[...]).astype(BF16)
    out_ref[0] = (x
                  + jnp.dot(mix_c, wo_ref[0:c3, :], preferred_element_type=F32)
                  + jnp.dot(mix_a, wo_ref[c3:c3 + ATTN_WIDTH, :], preferred_element_type=F32))


def _ffn_kernel(x_ref, g_ref, wg_ref, wu_ref, wd_ref, out_ref):
    x = x_ref[...]
    h = _rms(x, g_ref[...]).astype(BF16)
    gate = jnp.dot(h, wg_ref[...], preferred_element_type=F32)
    up = jnp.dot(h, wu_ref[...], preferred_element_type=F32)
    act = (gate * jax.nn.sigmoid(gate) * up).astype(BF16)
    out_ref[...] = x + jnp.dot(act, wd_ref[...], preferred_element_type=F32)


def _resident(shape, index_map):
    return pl.BlockSpec(shape, index_map, pipeline_mode=pl.Buffered(1))


def _mixer(x, g1, w_in, conv_w, qg2, kg2, sink_rows, cog, aog, w_o, layer, tm):
    b, seq, d = x.shape
    lsel3 = lambda bi, ji: (layer, 0, 0)
    vec = lambda n: _resident((None, 1, n), lsel3)
    n_units = (tm // BLOCK) * N_KV_HEADS * 2
    return pl.pallas_call(
        functools.partial(_mixer_kernel, tm=tm),
        name=f"mixer_l{layer}",
        grid=(b, seq // tm),
        in_specs=[
            pl.BlockSpec((1, tm, d), lambda bi, ji: (bi, ji, 0)),
            vec(d),
            _resident((None, d, IN_COLS), lsel3),
            _resident((None, SUBLANES, CONV_CHANNELS), lsel3),
            vec(LANES),
            vec(LANES),
            _resident((None, SUBLANES, COLS_PER_KV * BLOCK), lsel3),
            vec(CONV_CHANNELS),
            vec(ATTN_WIDTH),
            _resident((None, CONV_CHANNELS + ATTN_WIDTH, d), lsel3),
        ],
        out_specs=pl.BlockSpec((1, tm, d), lambda bi, ji: (bi, ji, 0)),
        out_shape=jax.ShapeDtypeStruct(x.shape, x.dtype),
        scratch_shapes=[
            pltpu.VMEM((tm + SUBLANES, CONV_CHANNELS), F32),
            pltpu.VMEM((N_KV_HEADS, COLS_PER_KV * tm, LANES), BF16),
            pltpu.VMEM((4, tm + BLOCK, LANES), BF16),
            pltpu.VMEM((4, LANES, tm + BLOCK), BF16),
            pltpu.VMEM((n_units, 2 * BLOCK, COLS_PER_KV * BLOCK), F32),
            pltpu.VMEM((n_units, 2 * BLOCK, COLS_PER_KV * BLOCK), BF16),
            pltpu.VMEM((tm, ATTN_WIDTH), F32),
            pltpu.VMEM((tm, CONV_CHANNELS), F32),
        ],
        compiler_params=pltpu.CompilerParams(
            dimension_semantics=("arbitrary", "arbitrary"),
            vmem_limit_bytes=VMEM_LIMIT_BYTES,
        ),
    )(x, g1, w_in, conv_w, qg2, kg2, sink_rows, cog, aog, w_o)


def _ffn(x2d, g, w_gate, w_up, w_down, layer, tm):
    n, d = x2d.shape
    lsel3 = lambda i: (layer, 0, 0)
    return pl.pallas_call(
        _ffn_kernel,
        name=f"ffn_l{layer}",
        grid=(n // tm,),
        in_specs=[
            pl.BlockSpec((tm, d), lambda i: (i, 0)),
            _resident((None, 1, d), lsel3),
            _resident((None, d, D_FF), lsel3),
            _resident((None, d, D_FF), lsel3),
            _resident((None, D_FF, d), lsel3),
        ],
        out_specs=pl.BlockSpec((tm, d), lambda i: (i, 0)),
        out_shape=jax.ShapeDtypeStruct(x2d.shape, x2d.dtype),
        compiler_params=pltpu.CompilerParams(
            dimension_semantics=("arbitrary",),
            vmem_limit_bytes=VMEM_LIMIT_BYTES,
        ),
    )(x2d, g, w_gate, w_up, w_down)


def kernel(x, norm1_g, w_in, conv_w, q_norm_g, k_norm_g, sinks, conv_out_g, attn_out_g, w_o, norm2_g,
           w_gate, w_up, w_down):
    b, seq, d = x.shape
    depth = w_in.shape[0]
    assert seq % TM_MIXER == 0 and (b * seq) % TM_FFN == 0 and TM_MIXER % BLOCK == 0

    row3 = lambda a: a.reshape(depth, 1, a.shape[-1])
    qg2 = row3(jnp.tile(q_norm_g * (HEAD_DIM ** -0.5 * LOG2E), (1, 2)))
    sink_rows = jnp.repeat((sinks * LOG2E).reshape(depth, N_KV_HEADS, COLS_PER_KV, 2).transpose(0, 1, 3, 2)
                           .reshape(depth, 2 * N_KV_HEADS, COLS_PER_KV), BLOCK, axis=-1)
    sink_rows = jnp.pad(sink_rows, ((0, 0), (0, SUBLANES - 2 * N_KV_HEADS), (0, 0)))
    kg2 = row3(jnp.tile(k_norm_g, (1, 2)))
    conv_w8 = jnp.pad(conv_w, ((0, 0), (0, SUBLANES - CONV_WIDTH), (0, 0)))
    w_in_b, w_o_b = w_in.astype(BF16), w_o.astype(BF16)
    w_gate_b, w_up_b, w_down_b = w_gate.astype(BF16), w_up.astype(BF16), w_down.astype(BF16)
    g1, g2 = row3(norm1_g), row3(norm2_g)
    cog, aog = row3(conv_out_g), row3(attn_out_g)

    for l in range(depth):
        x = _mixer(x, g1, w_in_b, conv_w8, qg2, kg2, sink_rows, cog, aog, w_o_b, l, TM_MIXER)
        x = _ffn(x.reshape(b * seq, d), g2, w_gate_b, w_up_b, w_down_b, l, TM_FFN).reshape(b, seq, d)
    return x
```

```python
import functools

import jax
import jax.numpy as jnp
from jax import lax
from jax.experimental import pallas as pl
from jax.experimental.pallas import tpu as pltpu

D_MODEL = 1024
CONV_CHANNELS = 512
CONV_WIDTH = 3
N_Q_HEADS = 8
N_KV_HEADS = 2
HEAD_DIM = 64
ATTN_WIDTH = N_Q_HEADS * HEAD_DIM
BLOCK = 128
IN_COLS = 3 * CONV_CHANNELS + (N_Q_HEADS + 2 * N_KV_HEADS) * HEAD_DIM
D_FF = 2816
EPS = 1e-6
NEG_INF = -1e30

LANES = 128
SUBLANES = 8
Q_COLS = ATTN_WIDTH // LANES
COLS_PER_KV = Q_COLS // N_KV_HEADS
LOG2E = 1.4426950408889634
KV_OFF = 3 * CONV_CHANNELS
VMEM_LIMIT_BYTES = 56 * 1024 * 1024

TM_MIXER = 1024
TM_FFN = 512

F32 = jnp.float32
BF16 = jnp.bfloat16


def _rms(x, g):
    ms = jnp.mean(x * x, axis=-1, keepdims=True)
    return x * lax.rsqrt(ms + EPS) * g


def _mixer_kernel(x_ref, g1_ref, win_ref, convw_ref, qg_ref, kg_ref, sink_ref, cog_ref, aog_ref, wo_ref,
                  out_ref, ubuf, qq, kk, vt, sbuf, pbuf, abuf, cbuf, *, tm):
    j = pl.program_id(1)
    c3 = CONV_CHANNELS
    nblk = tm // BLOCK

    @pl.when(j == 0)
    def _():
        ubuf[0:SUBLANES, :] = jnp.zeros((SUBLANES, c3), F32)
        kk[:, 0:BLOCK, :] = jnp.zeros((4, BLOCK, LANES), BF16)
        vt[:, :, 0:BLOCK] = jnp.zeros((4, LANES, BLOCK), BF16)

    x = x_ref[0]
    h = _rms(x, g1_ref[...]).astype(BF16)

    def conv_chunk(lo, hi):
        b_gate = jnp.dot(h, win_ref[:, lo:hi], preferred_element_type=F32)
        u = (jnp.dot(h, win_ref[:, c3 + lo:c3 + hi], preferred_element_type=F32)
             * jnp.dot(h, win_ref[:, 2 * c3 + lo:2 * c3 + hi], preferred_element_type=F32))
        ubuf[SUBLANES:SUBLANES + tm, lo:hi] = u
        y = (convw_ref[0:1, lo:hi] * ubuf[SUBLANES - 2:SUBLANES - 2 + tm, lo:hi]
             + convw_ref[1:2, lo:hi] * ubuf[SUBLANES - 1:SUBLANES - 1 + tm, lo:hi]
             + convw_ref[2:3, lo:hi] * u)
        ubuf[0:SUBLANES, lo:hi] = ubuf[tm:tm + SUBLANES, lo:hi]
        cbuf[:, lo:hi] = b_gate * y

    qkv = jnp.dot(h, win_ref[:, KV_OFF:IN_COLS], preferred_element_type=F32)
    lane = lax.broadcasted_iota(jnp.int32, (1, LANES), 1)
    lo_half = lane < HEAD_DIM

    def head_norm(t, g2):
        sq = t * t
        s_lo = jnp.sum(jnp.where(lo_half, sq, 0.0), axis=-1, keepdims=True)
        s_hi = jnp.sum(jnp.where(lo_half, 0.0, sq), axis=-1, keepdims=True)
        ms = jnp.where(lo_half, s_lo, s_hi) * (1.0 / HEAD_DIM)
        return t * lax.rsqrt(ms + EPS) * g2

    conv_chunk(0, c3 // 2)
    conv_chunk(c3 // 2, c3)

    for c in range(Q_COLS):
        qn = head_norm(qkv[:, c * LANES:(c + 1) * LANES], qg_ref[...]).astype(BF16)
        g, r = divmod(c, COLS_PER_KV)
        for b in range(nblk):
            qq[g, (COLS_PER_KV * b + r) * BLOCK:(COLS_PER_KV * b + r + 1) * BLOCK, :] = qn[b * BLOCK:(b + 1) * BLOCK, :]

    kn = head_norm(qkv[:, ATTN_WIDTH:ATTN_WIDTH + LANES], kg_ref[...])
    k_a0 = jnp.where(lo_half, kn, 0.0)
    k_b1 = jnp.where(lo_half, 0.0, kn)
    kk[0, BLOCK:BLOCK + tm, :] = k_a0.astype(BF16)
    kk[1, BLOCK:BLOCK + tm, :] = pltpu.roll(k_a0, HEAD_DIM, 1).astype(BF16)
    kk[2, BLOCK:BLOCK + tm, :] = pltpu.roll(k_b1, HEAD_DIM, 1).astype(BF16)
    kk[3, BLOCK:BLOCK + tm, :] = k_b1.astype(BF16)

    v_t = qkv[:, ATTN_WIDTH + LANES:ATTN_WIDTH + 2 * LANES].T
    ones_row = (lax.broadcasted_iota(jnp.int32, (HEAD_DIM, tm), 0) == 0).astype(F32)
    for g in range(N_KV_HEADS):
        v_g = v_t[g * HEAD_DIM:(g + 1) * HEAD_DIM, :]
        vt[2 * g, :, BLOCK:BLOCK + tm] = jnp.concatenate([v_g, ones_row], axis=0).astype(BF16)
        vt[2 * g + 1, :, BLOCK:BLOCK + tm] = jnp.concatenate([ones_row, v_g], axis=0).astype(BF16)

    pairs = [(b, g) for b in range(nblk) for g in range(N_KV_HEADS)]

    key_i = lax.broadcasted_iota(jnp.int32, (2 * BLOCK, COLS_PER_KV * BLOCK), 0)
    qry_i = lax.broadcasted_iota(jnp.int32, (2 * BLOCK, COLS_PER_KV * BLOCK), 1) & (BLOCK - 1)
    band = (key_i > qry_i) & (key_i <= qry_i + BLOCK)
    first_band = band & (key_i >= jnp.where(j == 0, BLOCK, 0))
    cap = jnp.where(band, jnp.inf, NEG_INF)
    first_cap = jnp.where(first_band, jnp.inf, NEG_INF)

    def scores(i):
        b, g = pairs[i]
        qs = qq[g, COLS_PER_KV * b * BLOCK:COLS_PER_KV * (b + 1) * BLOCK, :]
        for half in range(2):
            keys = kk[2 * g + half, b * BLOCK:(b + 2) * BLOCK, :]
            sbuf[2 * i + half] = lax.dot_general(keys, qs, (((1,), (1,)), ((), ())), preferred_element_type=F32)

    sink_term = {}

    def softmax(i):
        b, g = pairs[i]
        for half in range(2):
            s = jnp.minimum(sbuf[2 * i + half], first_cap if b == 0 else cap)
            sink = sink_ref[2 * g + half:2 * g + half + 1, :]
            m = jnp.maximum(jnp.max(s, axis=0, keepdims=True), sink)
            sink_term[2 * i + half] = jnp.exp2(sink - m)
            pbuf[2 * i + half] = jnp.exp2(s - m).astype(BF16)

    def weighted_values(i):
        b, g = pairs[i]
        keys = slice(b * BLOCK, (b + 2) * BLOCK)
        o_lo = jnp.dot(vt[2 * g, :, keys], pbuf[2 * i], preferred_element_type=F32)
        o_hi = jnp.dot(vt[2 * g + 1, :, keys], pbuf[2 * i + 1], preferred_element_type=F32)
        inv_lo = 1.0 / (o_lo[HEAD_DIM:HEAD_DIM + 1, :] + sink_term[2 * i])
        inv_hi = 1.0 / (o_hi[0:1, :] + sink_term[2 * i + 1])
        o_t = jnp.concatenate([o_lo[0:HEAD_DIM, :] * inv_lo, o_hi[HEAD_DIM:2 * HEAD_DIM, :] * inv_hi], axis=0)
        for r in range(COLS_PER_KV):
            c = COLS_PER_KV * g + r
            abuf[b * BLOCK:(b + 1) * BLOCK, c * LANES:(c + 1) * LANES] = o_t[:, r * BLOCK:(r + 1) * BLOCK].T

    for i in range(len(pairs) + 2):
        if i < len(pairs):
            scores(i)
        if 0 <= i - 1 < len(pairs):
            softmax(i - 1)
        if i - 2 >= 0:
            weighted_values(i - 2)

    kk[:, 0:BLOCK, :] = kk[:, tm:tm + BLOCK, :]
    vt[:, :, 0:BLOCK] = vt[:, :, tm:tm + BLOCK]

    mix_c = _rms(cbuf[...], cog_ref[...]).astype(BF16)
    mix_a = _rms(abuf[...], aog_ref[...]).astype(BF16)
    out_ref[0] = (x
                  + jnp.dot(mix_c, wo_ref[0:c3, :], preferred_element_type=F32)
                  + jnp.dot(mix_a, wo_ref[c3:c3 + ATTN_WIDTH, :], preferred_element_type=F32))


def _ffn_kernel(x_ref, g_ref, wg_ref, wu_ref, wd_ref, out_ref):
    x = x_ref[...]
    h = _rms(x, g_ref[...]).astype(BF16)
    gate = jnp.dot(h, wg_ref[...], preferred_element_type=F32)
    up = jnp.dot(h, wu_ref[...], preferred_element_type=F32)
    act = (gate * jax.nn.sigmoid(gate) * up).astype(BF16)
    out_ref[...] = x + jnp.dot(act, wd_ref[...], preferred_element_type=F32)


def _resident(shape, index_map):
    return pl.BlockSpec(shape, index_map, pipeline_mode=pl.Buffered(1))


def _mixer(x, g1, w_in, conv_w, qg2, kg2, sink_rows, cog, aog, w_o, layer, tm):
    b, seq, d = x.shape
    lsel3 = lambda bi, ji: (layer, 0, 0)
    vec = lambda n: _resident((None, 1, n), lsel3)
    n_units = (tm // BLOCK) * N_KV_HEADS * 2
    return pl.pallas_call(
        functools.partial(_mixer_kernel, tm=tm),
        name=f"mixer_l{layer}",
        grid=(b, seq // tm),
        in_specs=[
            pl.BlockSpec((1, tm, d), lambda bi, ji: (bi, ji, 0)),
            vec(d),
            _resident((None, d, IN_COLS), lsel3),
            _resident((None, SUBLANES, CONV_CHANNELS), lsel3),
            vec(LANES),
            vec(LANES),
            _resident((None, SUBLANES, COLS_PER_KV * BLOCK), lsel3),
            vec(CONV_CHANNELS),
            vec(ATTN_WIDTH),
            _resident((None, CONV_CHANNELS + ATTN_WIDTH, d), lsel3),
        ],
        out_specs=pl.BlockSpec((1, tm, d), lambda bi, ji: (bi, ji, 0)),
        out_shape=jax.ShapeDtypeStruct(x.shape, x.dtype),
        scratch_shapes=[
            pltpu.VMEM((tm + SUBLANES, CONV_CHANNELS), F32),
            pltpu.VMEM((N_KV_HEADS, COLS_PER_KV * tm, LANES), BF16),
            pltpu.VMEM((4, tm + BLOCK, LANES), BF16),
            pltpu.VMEM((4, LANES, tm + BLOCK), BF16),
            pltpu.VMEM((n_units, 2 * BLOCK, COLS_PER_KV * BLOCK), F32),
            pltpu.VMEM((n_units, 2 * BLOCK, COLS_PER_KV * BLOCK), BF16),
            pltpu.VMEM((tm, ATTN_WIDTH), F32),
            pltpu.VMEM((tm, CONV_CHANNELS), F32),
        ],
        compiler_params=pltpu.CompilerParams(
            dimension_semantics=("arbitrary", "arbitrary"),
            vmem_limit_bytes=VMEM_LIMIT_BYTES,
        ),
    )(x, g1, w_in, conv_w, qg2, kg2, sink_rows, cog, aog, w_o)


def _ffn(x2d, g, w_gate, w_up, w_down, layer, tm):
    n, d = x2d.shape
    lsel3 = lambda i: (layer, 0, 0)
    return pl.pallas_call(
        _ffn_kernel,
        name=f"ffn_l{layer}",
        grid=(n // tm,),
        in_specs=[
            pl.BlockSpec((tm, d), lambda i: (i, 0)),
            _resident((None, 1, d), lsel3),
            _resident((None, d, D_FF), lsel3),
            _resident((None, d, D_FF), lsel3),
            _resident((None, D_FF, d), lsel3),
        ],
        out_specs=pl.BlockSpec((tm, d), lambda i: (i, 0)),
        out_shape=jax.ShapeDtypeStruct(x2d.shape, x2d.dtype),
        compiler_params=pltpu.CompilerParams(
            dimension_semantics=("arbitrary",),
            vmem_limit_bytes=VMEM_LIMIT_BYTES,
        ),
    )(x2d, g, w_gate, w_up, w_down)


def kernel(x, norm1_g, w_in, conv_w, q_norm_g, k_norm_g, sinks, conv_out_g, attn_out_g, w_o, norm2_g,
           w_gate, w_up, w_down):
    b, seq, d = x.shape
    depth = w_in.shape[0]
    assert seq % TM_MIXER == 0 and (b * seq) % TM_FFN == 0 and TM_MIXER % BLOCK == 0

    row3 = lambda a: a.reshape(depth, 1, a.shape[-1])
    qg2 = row3(jnp.tile(q_norm_g * (HEAD_DIM ** -0.5 * LOG2E), (1, 2)))
    sink_rows = jnp.repeat((sinks * LOG2E).reshape(depth, N_KV_HEADS, COLS_PER_KV, 2).transpose(0, 1, 3, 2)
                           .reshape(depth, 2 * N_KV_HEADS, COLS_PER_KV), BLOCK, axis=-1)
    sink_rows = jnp.pad(sink_rows, ((0, 0), (0, SUBLANES - 2 * N_KV_HEADS), (0, 0)))
    kg2 = row3(jnp.tile(k_norm_g, (1, 2)))
    conv_w8 = jnp.pad(conv_w, ((0, 0), (0, SUBLANES - CONV_WIDTH), (0, 0)))
    w_in_b, w_o_b = w_in.astype(BF16), w_o.astype(BF16)
    w_gate_b, w_up_b, w_down_b = w_gate.astype(BF16), w_up.astype(BF16), w_down.astype(BF16)
    g1, g2 = row3(norm1_g), row3(norm2_g)
    cog, aog = row3(conv_out_g), row3(attn_out_g)

    for l in range(depth):
        x = _mixer(x, g1, w_in_b, conv_w8, qg2, kg2, sink_rows, cog, aog, w_o_b, l, TM_MIXER)
        x = _ffn(x.reshape(b * seq, d), g2, w_gate_b, w_up_b, w_down_b, l, TM_FFN).reshape(b, seq, d)
    return x
```

```python
import functools

import jax
import jax.numpy as jnp
from jax import lax
from jax.experimental import pallas as pl
from jax.experimental.pallas import tpu as pltpu

D_MODEL = 1024
CONV_CHANNELS = 512
CONV_WIDTH = 3
N_Q_HEADS = 8
N_KV_HEADS = 2
HEAD_DIM = 64
ATTN_WIDTH = N_Q_HEADS * HEAD_DIM
BLOCK = 128
IN_COLS = 3 * CONV_CHANNELS + (N_Q_HEADS + 2 * N_KV_HEADS) * HEAD_DIM
D_FF = 2816
EPS = 1e-6
NEG_INF = -1e30

LANES = 128
SUBLANES = 8
Q_COLS = ATTN_WIDTH // LANES
COLS_PER_KV = Q_COLS // N_KV_HEADS
LOG2E = 1.4426950408889634
KV_OFF = 3 * CONV_CHANNELS
VMEM_LIMIT_BYTES = 56 * 1024 * 1024

TM_MIXER = 1024
N_MIXER_IN = 10
TM_FFN = 1024
FF_ROWS = 512
FF_CHUNK = 1024

F32 = jnp.float32
BF16 = jnp.bfloat16


def _rms(x, g):
    ms = jnp.mean(x * x, axis=-1, keepdims=True)
    return x * lax.rsqrt(ms + EPS) * g


def _mixer_kernel(*refs, tm, n_cast):
    (x_ref, g1_ref, win_ref, convw_ref, qg_ref, kg_ref, sink_ref, cog_ref, aog_ref, wo_ref) = refs[:N_MIXER_IN]
    cast_src = refs[N_MIXER_IN:N_MIXER_IN + n_cast]
    out_ref = refs[N_MIXER_IN + n_cast]
    cast_dst = refs[N_MIXER_IN + n_cast + 1:N_MIXER_IN + 2 * n_cast + 1]
    ubuf, qq, kk, vt, sbuf, pbuf, abuf, cbuf = refs[N_MIXER_IN + 2 * n_cast + 1:]
    j = pl.program_id(1)

    for src, dst in zip(cast_src, cast_dst):
        dst[...] = src[...].astype(BF16)
    c3 = CONV_CHANNELS
    nblk = tm // BLOCK

    @pl.when(j == 0)
    def _():
        ubuf[0:SUBLANES, :] = jnp.zeros((SUBLANES, c3), F32)
        kk[:, 0:BLOCK, :] = jnp.zeros((4, BLOCK, LANES), BF16)
        vt[:, :, 0:BLOCK] = jnp.zeros((4, LANES, BLOCK), BF16)

    h = _rms(x_ref[0], g1_ref[...]).astype(BF16)

    def in_proj(lo, hi):
        return jnp.dot(h, win_ref[:, lo:hi], preferred_element_type=F32)

    def conv_chunk(lo, hi):
        b_gate = in_proj(lo, hi)
        u = in_proj(c3 + lo, c3 + hi) * in_proj(2 * c3 + lo, 2 * c3 + hi)
        ubuf[SUBLANES:SUBLANES + tm, lo:hi] = u
        y = (convw_ref[0:1, lo:hi] * ubuf[SUBLANES - 2:SUBLANES - 2 + tm, lo:hi]
             + convw_ref[1:2, lo:hi] * ubuf[SUBLANES - 1:SUBLANES - 1 + tm, lo:hi]
             + convw_ref[2:3, lo:hi] * u)
        ubuf[0:SUBLANES, lo:hi] = ubuf[tm:tm + SUBLANES, lo:hi]
        cbuf[:, lo:hi] = b_gate * y

    qkv = in_proj(KV_OFF, IN_COLS)
    lane = lax.broadcasted_iota(jnp.int32, (1, LANES), 1)
    lo_half = lane < HEAD_DIM

    def head_norm(t, g2):
        sq = t * t
        s_lo = jnp.sum(jnp.where(lo_half, sq, 0.0), axis=-1, keepdims=True)
        s_hi = jnp.sum(jnp.where(lo_half, 0.0, sq), axis=-1, keepdims=True)
        ms = jnp.where(lo_half, s_lo, s_hi) * (1.0 / HEAD_DIM)
        return t * lax.rsqrt(ms + EPS) * g2

    conv_chunk(0, c3 // 2)
    conv_chunk(c3 // 2, c3)

    for c in range(Q_COLS):
        qn = head_norm(qkv[:, c * LANES:(c + 1) * LANES], qg_ref[...]).astype(BF16)
        g, r = divmod(c, COLS_PER_KV)
        for b in range(nblk):
            qq[g, (COLS_PER_KV * b + r) * BLOCK:(COLS_PER_KV * b + r + 1) * BLOCK, :] = qn[b * BLOCK:(b + 1) * BLOCK, :]

    kn = head_norm(qkv[:, ATTN_WIDTH:ATTN_WIDTH + LANES], kg_ref[...])
    k_a0 = jnp.where(lo_half, kn, 0.0)
    k_b1 = jnp.where(lo_half, 0.0, kn)
    kk[0, BLOCK:BLOCK + tm, :] = k_a0.astype(BF16)
    kk[1, BLOCK:BLOCK + tm, :] = pltpu.roll(k_a0, HEAD_DIM, 1).astype(BF16)
    kk[2, BLOCK:BLOCK + tm, :] = pltpu.roll(k_b1, HEAD_DIM, 1).astype(BF16)
    kk[3, BLOCK:BLOCK + tm, :] = k_b1.astype(BF16)

    v_t = qkv[:, ATTN_WIDTH + LANES:ATTN_WIDTH + 2 * LANES].T
    ones_row = (lax.broadcasted_iota(jnp.int32, (HEAD_DIM, tm), 0) == 0).astype(F32)
    for g in range(N_KV_HEADS):
        v_g = v_t[g * HEAD_DIM:(g + 1) * HEAD_DIM, :]
        vt[2 * g, :, BLOCK:BLOCK + tm] = jnp.concatenate([v_g, ones_row], axis=0).astype(BF16)
        vt[2 * g + 1, :, BLOCK:BLOCK + tm] = jnp.concatenate([ones_row, v_g], axis=0).astype(BF16)

    pairs = [(b, g) for b in range(nblk) for g in range(N_KV_HEADS)]

    key_i = lax.broadcasted_iota(jnp.int32, (2 * BLOCK, COLS_PER_KV * BLOCK), 0)
    qry_i = lax.broadcasted_iota(jnp.int32, (2 * BLOCK, COLS_PER_KV * BLOCK), 1) & (BLOCK - 1)
    band = (key_i > qry_i) & (key_i <= qry_i + BLOCK)
    first_band = band & (key_i >= jnp.where(j == 0, BLOCK, 0))
    cap = jnp.where(band, jnp.inf, NEG_INF)
    first_cap = jnp.where(first_band, jnp.inf, NEG_INF)

    def scores(i):
        b, g = pairs[i]
        qs = qq[g, COLS_PER_KV * b * BLOCK:COLS_PER_KV * (b + 1) * BLOCK, :]
        for half in range(2):
            keys = kk[2 * g + half, b * BLOCK:(b + 2) * BLOCK, :]
            sbuf[2 * i + half] = lax.dot_general(keys, qs, (((1,), (1,)), ((), ())), preferred_element_type=F32)

    sink_term = {}

    def softmax(i):
        b, g = pairs[i]
        for half in range(2):
            s = jnp.minimum(sbuf[2 * i + half], first_cap if b == 0 else cap)
            sink = sink_ref[2 * g + half:2 * g + half + 1, :]
            m = jnp.maximum(jnp.max(s, axis=0, keepdims=True), sink)
            sink_term[2 * i + half] = jnp.exp2(sink - m)
            pbuf[2 * i + half] = jnp.exp2(s - m).astype(BF16)

    def weighted_values(i):
        b, g = pairs[i]
        keys = slice(b * BLOCK, (b + 2) * BLOCK)
        o_lo = jnp.dot(vt[2 * g, :, keys], pbuf[2 * i], preferred_element_type=F32)
        o_hi = jnp.dot(vt[2 * g + 1, :, keys], pbuf[2 * i + 1], preferred_element_type=F32)
        inv_lo = 1.0 / (o_lo[HEAD_DIM:HEAD_DIM + 1, :] + sink_term[2 * i])
        inv_hi = 1.0 / (o_hi[0:1, :] + sink_term[2 * i + 1])
        o_t = jnp.concatenate([o_lo[0:HEAD_DIM, :] * inv_lo, o_hi[HEAD_DIM:2 * HEAD_DIM, :] * inv_hi], axis=0)
        for r in range(COLS_PER_KV):
            c = COLS_PER_KV * g + r
            abuf[b * BLOCK:(b + 1) * BLOCK, c * LANES:(c + 1) * LANES] = o_t[:, r * BLOCK:(r + 1) * BLOCK].T

    for i in range(len(pairs) + 2):
        if i < len(pairs):
            scores(i)
        if 0 <= i - 1 < len(pairs):
            softmax(i - 1)
        if i - 2 >= 0:
            weighted_values(i - 2)

    kk[:, 0:BLOCK, :] = kk[:, tm:tm + BLOCK, :]
    vt[:, :, 0:BLOCK] = vt[:, :, tm:tm + BLOCK]

    mix_c = _rms(cbuf[...], cog_ref[...]).astype(BF16)
    mix_a = _rms(abuf[...], aog_ref[...]).astype(BF16)
    out_ref[0] = (x_ref[0]
                  + jnp.dot(mix_c, wo_ref[0:c3, :], preferred_element_type=F32)
                  + jnp.dot(mix_a, wo_ref[c3:c3 + ATTN_WIDTH, :], preferred_element_type=F32))


def _ffn_kernel(x_ref, g_ref, wg_ref, wu_ref, wd_ref, out_ref):
    n_sub = x_ref.shape[0] // FF_ROWS
    chunks = [(lo, min(lo + FF_CHUNK, D_FF)) for lo in range(0, D_FF, FF_CHUNK)]
    items = [(r, c) for r in range(n_sub) for c in range(len(chunks))]

    def rows(r):
        return slice(r * FF_ROWS, (r + 1) * FF_ROWS)

    def normed(r):
        return _rms(x_ref[rows(r), :], g_ref[...]).astype(BF16)

    def gate_up(h, c):
        lo, hi = chunks[c]
        return (jnp.dot(h, wg_ref[:, lo:hi], preferred_element_type=F32),
                jnp.dot(h, wu_ref[:, lo:hi], preferred_element_type=F32))

    h = {0: normed(0)}
    nxt = gate_up(h[0], 0)
    acc = None
    for k, (r, c) in enumerate(items):
        gate, up = nxt
        if c == 0:
            acc = x_ref[rows(r), :]
            if r + 1 < n_sub:
                h[r + 1] = normed(r + 1)
        if k + 1 < len(items):
            r1, c1 = items[k + 1]
            nxt = gate_up(h[r1], c1)
        act = (gate * jax.nn.sigmoid(gate) * up).astype(BF16)
        lo, hi = chunks[c]
        acc = acc + jnp.dot(act, wd_ref[lo:hi, :], preferred_element_type=F32)
        if c == len(chunks) - 1:
            out_ref[rows(r), :] = acc


def _resident(shape, index_map):
    return pl.BlockSpec(shape, index_map, pipeline_mode=pl.Buffered(1))


def _mixer(x, g1, w_in, conv_w, qg2, kg2, sink_rows, cog, aog, w_o, to_cast, layer, tm):
    b, seq, d = x.shape
    steps = b * (seq // tm)
    lsel3 = lambda bi, ji: (layer, 0, 0)
    top2 = lambda bi, ji: (0, 0)
    vec = lambda n: _resident((None, 1, n), lsel3)
    n_units = (tm // BLOCK) * N_KV_HEADS * 2
    cast_in, cast_out, cast_shapes = [], [], []
    for w, wl in to_cast:
        rows, cols = w.shape[1] // steps, w.shape[2]
        assert rows * steps == w.shape[1] and rows % (2 * SUBLANES) == 0
        cast_in.append(pl.BlockSpec((None, rows, cols), lambda bi, ji, wl=wl: (wl, bi * (seq // tm) + ji, 0)))
        cast_out.append(pl.BlockSpec((rows, cols), lambda bi, ji: (bi * (seq // tm) + ji, 0)))
        cast_shapes.append(jax.ShapeDtypeStruct(w.shape[1:], BF16))
    outs = pl.pallas_call(
        functools.partial(_mixer_kernel, tm=tm, n_cast=len(to_cast)),
        name=f"mixer_l{layer}",
        grid=(b, seq // tm),
        in_specs=[
            pl.BlockSpec((1, tm, d), lambda bi, ji: (bi, ji, 0)),
            vec(d),
            _resident((d, IN_COLS), top2),
            _resident((None, SUBLANES, CONV_CHANNELS), lsel3),
            vec(LANES),
            vec(LANES),
            _resident((None, SUBLANES, COLS_PER_KV * BLOCK), lsel3),
            vec(CONV_CHANNELS),
            vec(ATTN_WIDTH),
            _resident((CONV_CHANNELS + ATTN_WIDTH, d), top2),
        ] + cast_in,
        out_specs=[pl.BlockSpec((1, tm, d), lambda bi, ji: (bi, ji, 0))] + cast_out,
        out_shape=[jax.ShapeDtypeStruct(x.shape, x.dtype)] + cast_shapes,
        scratch_shapes=[
            pltpu.VMEM((tm + SUBLANES, CONV_CHANNELS), F32),
            pltpu.VMEM((N_KV_HEADS, COLS_PER_KV * tm, LANES), BF16),
            pltpu.VMEM((4, tm + BLOCK, LANES), BF16),
            pltpu.VMEM((4, LANES, tm + BLOCK), BF16),
            pltpu.VMEM((n_units, 2 * BLOCK, COLS_PER_KV * BLOCK), F32),
            pltpu.VMEM((n_units, 2 * BLOCK, COLS_PER_KV * BLOCK), BF16),
            pltpu.VMEM((tm, ATTN_WIDTH), F32),
            pltpu.VMEM((tm, CONV_CHANNELS), F32),
        ],
        compiler_params=pltpu.CompilerParams(
            dimension_semantics=("arbitrary", "arbitrary"),
            vmem_limit_bytes=VMEM_LIMIT_BYTES,
        ),
    )(x, g1, w_in, conv_w, qg2, kg2, sink_rows, cog, aog, w_o, *[w for w, _ in to_cast])
    return outs[0], outs[1:]


def _ffn(x2d, g, w_gate, w_up, w_down, layer, tm):
    n, d = x2d.shape
    lsel3 = lambda i: (layer, 0, 0)
    top2 = lambda i: (0, 0)
    return pl.pallas_call(
        _ffn_kernel,
        name=f"ffn_l{layer}",
        grid=(n // tm,),
        in_specs=[
            pl.BlockSpec((tm, d), lambda i: (i, 0)),
            _resident((None, 1, d), lsel3),
            _resident((d, D_FF), top2),
            _resident((d, D_FF), top2),
            _resident((D_FF, d), top2),
        ],
        out_specs=pl.BlockSpec((tm, d), lambda i: (i, 0)),
        out_shape=jax.ShapeDtypeStruct(x2d.shape, x2d.dtype),
        compiler_params=pltpu.CompilerParams(
            dimension_semantics=("arbitrary",),
            vmem_limit_bytes=VMEM_LIMIT_BYTES,
        ),
    )(x2d, g, w_gate, w_up, w_down)


def kernel(x, norm1_g, w_in, conv_w, q_norm_g, k_norm_g, sinks, conv_out_g, attn_out_g, w_o, norm2_g,
           w_gate, w_up, w_down):
    b, seq, d = x.shape
    depth = w_in.shape[0]
    assert seq % TM_MIXER == 0 and (b * seq) % TM_FFN == 0 and TM_MIXER % BLOCK == 0

    row3 = lambda a: a.reshape(depth, 1, a.shape[-1])
    qg2 = row3(jnp.tile(q_norm_g * (HEAD_DIM ** -0.5 * LOG2E), (1, 2)))
    sink_rows = jnp.repeat((sinks * LOG2E).reshape(depth, N_KV_HEADS, COLS_PER_KV, 2).transpose(0, 1, 3, 2)
                           .reshape(depth, 2 * N_KV_HEADS, COLS_PER_KV), BLOCK, axis=-1)
    sink_rows = jnp.pad(sink_rows, ((0, 0), (0, SUBLANES - 2 * N_KV_HEADS), (0, 0)))
    kg2 = row3(jnp.tile(k_norm_g, (1, 2)))
    conv_w8 = jnp.pad(conv_w, ((0, 0), (0, SUBLANES - CONV_WIDTH), (0, 0)))
    w_in_b, w_o_b = w_in[0].astype(BF16), w_o[0].astype(BF16)
    g1, g2 = row3(norm1_g), row3(norm2_g)
    cog, aog = row3(conv_out_g), row3(attn_out_g)

    for l in range(depth):
        to_cast = [(w_gate, l), (w_up, l), (w_down, l)]
        if l + 1 < depth:
            to_cast += [(w_in, l + 1), (w_o, l + 1)]
        x, cast = _mixer(x, g1, w_in_b, conv_w8, qg2, kg2, sink_rows, cog, aog, w_o_b, to_cast, l, TM_MIXER)
        w_gate_b, w_up_b, w_down_b = cast[:3]
        if l + 1 < depth:
            w_in_b, w_o_b = cast[3:]
        x = _ffn(x.reshape(b * seq, d), g2, w_gate_b, w_up_b, w_down_b, l, TM_FFN).reshape(b, seq, d)
    return x
```

```python
import functools

import jax
import jax.numpy as jnp
from jax import lax
from jax.experimental import pallas as pl
from jax.experimental.pallas import tpu as pltpu

D_MODEL = 1024
CONV_CHANNELS = 512
CONV_WIDTH = 3
N_Q_HEADS = 8
N_KV_HEADS = 2
HEAD_DIM = 64
ATTN_WIDTH = N_Q_HEADS * HEAD_DIM
BLOCK = 128
IN_COLS = 3 * CONV_CHANNELS + (N_Q_HEADS + 2 * N_KV_HEADS) * HEAD_DIM
D_FF = 2816
EPS = 1e-6
NEG_INF = -1e30

LANES = 128
SUBLANES = 8
Q_COLS = ATTN_WIDTH // LANES
COLS_PER_KV = Q_COLS // N_KV_HEADS
LOG2E = 1.4426950408889634
KV_OFF = 3 * CONV_CHANNELS
VMEM_LIMIT_BYTES = 56 * 1024 * 1024

TM_MIXER = 1024
CONV_CHUNK = 256
N_MIXER_IN = 11
TM_FFN = 1024
FF_ROWS = 512
FF_CHUNK = 1024

F32 = jnp.float32
BF16 = jnp.bfloat16


def _rms(x, g):
    ms = jnp.mean(x * x, axis=-1, keepdims=True)
    return x * lax.rsqrt(ms + EPS) * g


def _mixer_kernel(*refs, tm, n_cast):
    (x_ref, g1_ref, win_ref, wqt_ref, convw_ref, qgt_ref, kg_ref, sink_ref, cog_ref, aog_ref, wo_ref) = refs[:N_MIXER_IN]
    cast_src = refs[N_MIXER_IN:N_MIXER_IN + n_cast]
    out_ref = refs[N_MIXER_IN + n_cast]
    cast_dst = refs[N_MIXER_IN + n_cast + 1:N_MIXER_IN + 2 * n_cast + 1]
    ubuf, qq, kk, vt, sbuf, pbuf, abuf, cbuf = refs[N_MIXER_IN + 2 * n_cast + 1:]
    j = pl.program_id(1)

    for src, dst in zip(cast_src, cast_dst):
        dst[...] = src[...].astype(BF16)
    c3 = CONV_CHANNELS
    nblk = tm // BLOCK

    @pl.when(j == 0)
    def _():
        ubuf[0:SUBLANES, :] = jnp.zeros((SUBLANES, c3), F32)
        kk[:, 0:BLOCK, :] = jnp.zeros((4, BLOCK, LANES), BF16)
        vt[:, :, 0:BLOCK] = jnp.zeros((4, LANES, BLOCK), BF16)

    h = _rms(x_ref[0], g1_ref[...]).astype(BF16)

    def in_proj(lo, hi):
        return jnp.dot(h, win_ref[:, lo:hi], preferred_element_type=F32)

    def conv_finish(lo, hi, b_gate, c_gate, hc):
        u = c_gate * hc
        ubuf[SUBLANES:SUBLANES + tm, lo:hi] = u
        y = (convw_ref[0:1, lo:hi] * ubuf[SUBLANES - 2:SUBLANES - 2 + tm, lo:hi]
             + convw_ref[1:2, lo:hi] * ubuf[SUBLANES - 1:SUBLANES - 1 + tm, lo:hi]
             + convw_ref[2:3, lo:hi] * u)
        ubuf[0:SUBLANES, lo:hi] = ubuf[tm:tm + SUBLANES, lo:hi]
        cbuf[:, lo:hi] = b_gate * y

    conv_steps = []
    for lo in range(0, c3, CONV_CHUNK):
        parts = []
        for base in (0, c3, 2 * c3):
            conv_steps.append(lambda lo=lo, base=base, parts=parts: parts.append(
                in_proj(base + lo, base + lo + CONV_CHUNK)))
        conv_steps.append(lambda lo=lo, parts=parts: conv_finish(lo, lo + CONV_CHUNK, *parts))
    out_c = []
    conv_steps.append(lambda: out_c.append(
        x_ref[0] + jnp.dot(_rms(cbuf[...], cog_ref[...]).astype(BF16), wo_ref[0:c3, :], preferred_element_type=F32)))

    kv = in_proj(KV_OFF + ATTN_WIDTH, IN_COLS)
    q_t = lax.dot_general(wqt_ref[...], h, (((1,), (1,)), ((), ())), preferred_element_type=F32)
    lane = lax.broadcasted_iota(jnp.int32, (1, LANES), 1)
    lo_half = lane < HEAD_DIM

    k_sq = kv[:, 0:LANES] * kv[:, 0:LANES]
    ms_lo = jnp.sum(jnp.where(lo_half, k_sq, 0.0), axis=-1, keepdims=True)
    ms_hi = jnp.sum(jnp.where(lo_half, 0.0, k_sq), axis=-1, keepdims=True)
    kn = kv[:, 0:LANES] * lax.rsqrt(jnp.where(lo_half, ms_lo, ms_hi) * (1.0 / HEAD_DIM) + EPS) * kg_ref[...]
    k_a0 = jnp.where(lo_half, kn, 0.0)
    k_b1 = jnp.where(lo_half, 0.0, kn)
    kk[0, BLOCK:BLOCK + tm, :] = k_a0.astype(BF16)
    kk[1, BLOCK:BLOCK + tm, :] = pltpu.roll(k_a0, HEAD_DIM, 1).astype(BF16)
    kk[2, BLOCK:BLOCK + tm, :] = pltpu.roll(k_b1, HEAD_DIM, 1).astype(BF16)
    kk[3, BLOCK:BLOCK + tm, :] = k_b1.astype(BF16)

    for hd in range(N_Q_HEADS):
        c, hf = divmod(hd, 2)
        g, r = divmod(c, COLS_PER_KV)
        t = q_t[hd * HEAD_DIM:(hd + 1) * HEAD_DIM, :]
        inv = lax.rsqrt(jnp.mean(t * t, axis=0, keepdims=True) + EPS)
        for b in range(nblk):
            blk = slice(b * BLOCK, (b + 1) * BLOCK)
            qq[g, b, hf * HEAD_DIM:(hf + 1) * HEAD_DIM, r * BLOCK:(r + 1) * BLOCK] = (
                t[:, blk] * inv[:, blk] * qgt_ref[...]).astype(BF16)

    v_t = kv[:, LANES:2 * LANES].T
    ones_row = (lax.broadcasted_iota(jnp.int32, (HEAD_DIM, tm), 0) == 0).astype(F32)
    for g in range(N_KV_HEADS):
        v_g = v_t[g * HEAD_DIM:(g + 1) * HEAD_DIM, :]
        vt[2 * g, :, BLOCK:BLOCK + tm] = jnp.concatenate([v_g, ones_row], axis=0).astype(BF16)
        vt[2 * g + 1, :, BLOCK:BLOCK + tm] = jnp.concatenate([ones_row, v_g], axis=0).astype(BF16)

    pairs = [(b, g) for b in range(nblk) for g in range(N_KV_HEADS)]

    key_i = lax.broadcasted_iota(jnp.int32, (2 * BLOCK, COLS_PER_KV * BLOCK), 0)
    qry_i = lax.broadcasted_iota(jnp.int32, (2 * BLOCK, COLS_PER_KV * BLOCK), 1) & (BLOCK - 1)
    band = (key_i > qry_i) & (key_i <= qry_i + BLOCK)
    first_band = band & (key_i >= jnp.where(j == 0, BLOCK, 0))
    cap = jnp.where(band, jnp.inf, NEG_INF)
    first_cap = jnp.where(first_band, jnp.inf, NEG_INF)

    def scores(i):
        b, g = pairs[i]
        for half in range(2):
            keys = kk[2 * g + half, b * BLOCK:(b + 2) * BLOCK, :]
            sbuf[2 * i + half] = jnp.dot(keys, qq[g, b], preferred_element_type=F32)

    sink_term = {}

    def softmax(i):
        b, g = pairs[i]
        for half in range(2):
            s = jnp.minimum(sbuf[2 * i + half], first_cap if b == 0 else cap)
            sink = sink_ref[2 * g + half:2 * g + half + 1, :]
            m = jnp.maximum(jnp.max(s, axis=0, keepdims=True), sink)
            sink_term[2 * i + half] = jnp.exp2(sink - m)
            pbuf[2 * i + half] = jnp.exp2(s - m).astype(BF16)

    def weighted_values(i):
        b, g = pairs[i]
        keys = slice(b * BLOCK, (b + 2) * BLOCK)
        o_lo = jnp.dot(vt[2 * g, :, keys], pbuf[2 * i], preferred_element_type=F32)
        o_hi = jnp.dot(vt[2 * g + 1, :, keys], pbuf[2 * i + 1], preferred_element_type=F32)
        inv_lo = 1.0 / (o_lo[HEAD_DIM:HEAD_DIM + 1, :] + sink_term[2 * i])
        inv_hi = 1.0 / (o_hi[0:1, :] + sink_term[2 * i + 1])
        o_t = jnp.concatenate([o_lo[0:HEAD_DIM, :] * inv_lo, o_hi[HEAD_DIM:2 * HEAD_DIM, :] * inv_hi], axis=0)
        for r in range(COLS_PER_KV):
            c = COLS_PER_KV * g + r
            abuf[b * BLOCK:(b + 1) * BLOCK, c * LANES:(c + 1) * LANES] = o_t[:, r * BLOCK:(r + 1) * BLOCK].T

    n_iter, n_conv = len(pairs) + 2, len(conv_steps)
    for i in range(n_iter):
        while len(conv_steps) * n_iter > (n_iter - 1 - i) * n_conv:
            conv_steps.pop(0)()
        if i < len(pairs):
            scores(i)
        if 0 <= i - 1 < len(pairs):
            softmax(i - 1)
        if i - 2 >= 0:
            weighted_values(i - 2)

    kk[:, 0:BLOCK, :] = kk[:, tm:tm + BLOCK, :]
    vt[:, :, 0:BLOCK] = vt[:, :, tm:tm + BLOCK]

    mix_a = _rms(abuf[...], aog_ref[...]).astype(BF16)
    out_ref[0] = out_c[0] + jnp.dot(mix_a, wo_ref[c3:c3 + ATTN_WIDTH, :], preferred_element_type=F32)


def _ffn_kernel(x_ref, g_ref, wg_ref, wu_ref, wd_ref, out_ref):
    n_sub = x_ref.shape[0] // FF_ROWS
    chunks = [(lo, min(lo + FF_CHUNK, D_FF)) for lo in range(0, D_FF, FF_CHUNK)]
    items = [(r, c) for r in range(n_sub) for c in range(len(chunks))]

    def rows(r):
        return slice(r * FF_ROWS, (r + 1) * FF_ROWS)

    def normed(r):
        x = x_ref[rows(r), :]
        inv_rms = lax.rsqrt(jnp.mean(x * x, axis=-1, keepdims=True) + EPS)
        return (x * g_ref[...]).astype(BF16), inv_rms

    def gate_up(h, c):
        lo, hi = chunks[c]
        xg, inv_rms = h
        return (jnp.dot(xg, wg_ref[:, lo:hi], preferred_element_type=F32) * inv_rms,
                jnp.dot(xg, wu_ref[:, lo:hi], preferred_element_type=F32) * inv_rms)

    h = {0: normed(0)}
    nxt = gate_up(h[0], 0)
    acc = None
    for k, (r, c) in enumerate(items):
        gate, up = nxt
        if c == 0:
            acc = x_ref[rows(r), :]
            if r + 1 < n_sub:
                h[r + 1] = normed(r + 1)
        if k + 1 < len(items):
            r1, c1 = items[k + 1]
            nxt = gate_up(h[r1], c1)
        act = (gate * jax.nn.sigmoid(gate) * up).astype(BF16)
        lo, hi = chunks[c]
        acc = acc + jnp.dot(act, wd_ref[lo:hi, :], preferred_element_type=F32)
        if c == len(chunks) - 1:
            out_ref[rows(r), :] = acc


def _resident(shape, index_map):
    return pl.BlockSpec(shape, index_map, pipeline_mode=pl.Buffered(1))


def _mixer(x, g1, w_in, wq_t, conv_w, qg_t, kg2, sink_rows, cog, aog, w_o, to_cast, layer, tm):
    b, seq, d = x.shape
    steps = b * (seq // tm)
    lsel3 = lambda bi, ji: (layer, 0, 0)
    top2 = lambda bi, ji: (0, 0)
    vec = lambda n: _resident((None, 1, n), lsel3)
    n_units = (tm // BLOCK) * N_KV_HEADS * 2
    cast_in, cast_out, cast_shapes = [], [], []
    for w, wl in to_cast:
        rows, cols = w.shape[1] // steps, w.shape[2]
        assert rows * steps == w.shape[1] and rows % (2 * SUBLANES) == 0
        cast_in.append(pl.BlockSpec((None, rows, cols), lambda bi, ji, wl=wl: (wl, bi * (seq // tm) + ji, 0)))
        cast_out.append(pl.BlockSpec((rows, cols), lambda bi, ji: (bi * (seq // tm) + ji, 0)))
        cast_shapes.append(jax.ShapeDtypeStruct(w.shape[1:], BF16))
    outs = pl.pallas_call(
        functools.partial(_mixer_kernel, tm=tm, n_cast=len(to_cast)),
        name=f"mixer_l{layer}",
        grid=(b, seq // tm),
        in_specs=[
            pl.BlockSpec((1, tm, d), lambda bi, ji: (bi, ji, 0)),
            vec(d),
            _resident((d, IN_COLS), top2),
            _resident((None, ATTN_WIDTH, d), lsel3),
            _resident((None, SUBLANES, CONV_CHANNELS), lsel3),
            _resident((None, HEAD_DIM, LANES), lsel3),
            vec(LANES),
            _resident((None, SUBLANES, COLS_PER_KV * BLOCK), lsel3),
            vec(CONV_CHANNELS),
            vec(ATTN_WIDTH),
            _resident((CONV_CHANNELS + ATTN_WIDTH, d), top2),
        ] + cast_in,
        out_specs=[pl.BlockSpec((1, tm, d), lambda bi, ji: (bi, ji, 0))] + cast_out,
        out_shape=[jax.ShapeDtypeStruct(x.shape, x.dtype)] + cast_shapes,
        scratch_shapes=[
            pltpu.VMEM((tm + SUBLANES, CONV_CHANNELS), F32),
            pltpu.VMEM((N_KV_HEADS, tm // BLOCK, LANES, COLS_PER_KV * BLOCK), BF16),
            pltpu.VMEM((4, tm + BLOCK, LANES), BF16),
            pltpu.VMEM((4, LANES, tm + BLOCK), BF16),
            pltpu.VMEM((n_units, 2 * BLOCK, COLS_PER_KV * BLOCK), F32),
            pltpu.VMEM((n_units, 2 * BLOCK, COLS_PER_KV * BLOCK), BF16),
            pltpu.VMEM((tm, ATTN_WIDTH), F32),
            pltpu.VMEM((tm, CONV_CHANNELS), F32),
        ],
        compiler_params=pltpu.CompilerParams(
            dimension_semantics=("arbitrary", "arbitrary"),
            vmem_limit_bytes=VMEM_LIMIT_BYTES,
        ),
    )(x, g1, w_in, wq_t, conv_w, qg_t, kg2, sink_rows, cog, aog, w_o, *[w for w, _ in to_cast])
    return outs[0], outs[1:]


def _ffn(x2d, g, w_gate, w_up, w_down, layer, tm):
    n, d = x2d.shape
    lsel3 = lambda i: (layer, 0, 0)
    top2 = lambda i: (0, 0)
    return pl.pallas_call(
        _ffn_kernel,
        name=f"ffn_l{layer}",
        grid=(n // tm,),
        in_specs=[
            pl.BlockSpec((tm, d), lambda i: (i, 0)),
            _resident((None, 1, d), lsel3),
            _resident((d, D_FF), top2),
            _resident((d, D_FF), top2),
            _resident((D_FF, d), top2),
        ],
        out_specs=pl.BlockSpec((tm, d), lambda i: (i, 0)),
        out_shape=jax.ShapeDtypeStruct(x2d.shape, x2d.dtype),
        compiler_params=pltpu.CompilerParams(
            dimension_semantics=("arbitrary",),
            vmem_limit_bytes=VMEM_LIMIT_BYTES,
        ),
    )(x2d, g, w_gate, w_up, w_down)


def kernel(x, norm1_g, w_in, conv_w, q_norm_g, k_norm_g, sinks, conv_out_g, attn_out_g, w_o, norm2_g,
           w_gate, w_up, w_down):
    b, seq, d = x.shape
    depth = w_in.shape[0]
    assert seq % TM_MIXER == 0 and (b * seq) % TM_FFN == 0 and TM_MIXER % BLOCK == 0

    row3 = lambda a: a.reshape(depth, 1, a.shape[-1])
    qg_t = jnp.broadcast_to((q_norm_g * (HEAD_DIM ** -0.5 * LOG2E))[:, :, None], (depth, HEAD_DIM, LANES))
    wq_t = w_in[:, :, KV_OFF:KV_OFF + ATTN_WIDTH].transpose(0, 2, 1).astype(BF16)
    sink_rows = jnp.repeat((sinks * LOG2E).reshape(depth, N_KV_HEADS, COLS_PER_KV, 2).transpose(0, 1, 3, 2)
                           .reshape(depth, 2 * N_KV_HEADS, COLS_PER_KV), BLOCK, axis=-1)
    sink_rows = jnp.pad(sink_rows, ((0, 0), (0, SUBLANES - 2 * N_KV_HEADS), (0, 0)))
    kg2 = row3(jnp.tile(k_norm_g, (1, 2)))
    conv_w8 = jnp.pad(conv_w, ((0, 0), (0, SUBLANES - CONV_WIDTH), (0, 0)))
    w_in_b, w_o_b = w_in[0].astype(BF16), w_o[0].astype(BF16)
    g1, g2 = row3(norm1_g), row3(norm2_g)
    cog, aog = row3(conv_out_g), row3(attn_out_g)

    for l in range(depth):
        to_cast = [(w_gate, l), (w_up, l), (w_down, l)]
        if l + 1 < depth:
            to_cast += [(w_in, l + 1), (w_o, l + 1)]
        x, cast = _mixer(x, g1, w_in_b, wq_t, conv_w8, qg_t, kg2, sink_rows, cog, aog, w_o_b, to_cast, l, TM_MIXER)
        w_gate_b, w_up_b, w_down_b = cast[:3]
        if l + 1 < depth:
            w_in_b, w_o_b = cast[3:]
        x = _ffn(x.reshape(b * seq, d), g2, w_gate_b, w_up_b, w_down_b, l, TM_FFN).reshape(b, seq, d)
    return x
```

```python
import functools

import jax
import jax.numpy as jnp
from jax import lax
from jax.experimental import pallas as pl
from jax.experimental.pallas import tpu as pltpu

D_MODEL = 1024
CONV_CHANNELS = 512
CONV_WIDTH = 3
N_Q_HEADS = 8
N_KV_HEADS = 2
HEAD_DIM = 64
ATTN_WIDTH = N_Q_HEADS * HEAD_DIM
BLOCK = 128
IN_COLS = 3 * CONV_CHANNELS + (N_Q_HEADS + 2 * N_KV_HEADS) * HEAD_DIM
D_FF = 2816
EPS = 1e-6
NEG_INF = -1e30

LANES = 128
SUBLANES = 8
Q_COLS = ATTN_WIDTH // LANES
COLS_PER_KV = Q_COLS // N_KV_HEADS
LOG2E = 1.4426950408889634
KV_OFF = 3 * CONV_CHANNELS
VMEM_LIMIT_BYTES = 56 * 1024 * 1024

TM_MIXER = 1024
CONV_CHUNK = 256
N_MIXER_IN = 10
TM_FFN = 1024
FF_ROWS = 512
FF_CHUNK = 1024

F32 = jnp.float32
BF16 = jnp.bfloat16


def _rms(x, g):
    ms = jnp.mean(x * x, axis=-1, keepdims=True)
    return x * lax.rsqrt(ms + EPS) * g


def _mixer_kernel(*refs, tm, n_cast):
    (x_ref, g1_ref, win_ref, convw_ref, qgt_ref, kg_ref, sink_ref, cog_ref, aog_ref, wo_ref) = refs[:N_MIXER_IN]
    cast_src = refs[N_MIXER_IN:N_MIXER_IN + n_cast]
    out_ref = refs[N_MIXER_IN + n_cast]
    cast_dst = refs[N_MIXER_IN + n_cast + 1:N_MIXER_IN + 2 * n_cast + 1]
    ubuf, qq, kk, vt, sbuf, pbuf, abuf, cbuf, wqt = refs[N_MIXER_IN + 2 * n_cast + 1:]
    j = pl.program_id(1)

    for src, dst in zip(cast_src, cast_dst):
        dst[...] = src[...].astype(BF16)
    c3 = CONV_CHANNELS
    nblk = tm // BLOCK

    @pl.when((pl.program_id(0) == 0) & (j == 0))
    def _():
        wqt[...] = win_ref[:, KV_OFF:KV_OFF + ATTN_WIDTH].astype(F32).T.astype(BF16)

    @pl.when(j == 0)
    def _():
        ubuf[0:SUBLANES, :] = jnp.zeros((SUBLANES, c3), F32)
        kk[:, 0:BLOCK, :] = jnp.zeros((4, BLOCK, LANES), BF16)
        vt[:, :, 0:BLOCK] = jnp.zeros((4, LANES, BLOCK), BF16)

    h = _rms(x_ref[0], g1_ref[...]).astype(BF16)

    def in_proj(lo, hi):
        return jnp.dot(h, win_ref[:, lo:hi], preferred_element_type=F32)

    def conv_finish(lo, hi, b_gate, c_gate, hc):
        u = c_gate * hc
        ubuf[SUBLANES:SUBLANES + tm, lo:hi] = u
        y = (convw_ref[0:1, lo:hi] * ubuf[SUBLANES - 2:SUBLANES - 2 + tm, lo:hi]
             + convw_ref[1:2, lo:hi] * ubuf[SUBLANES - 1:SUBLANES - 1 + tm, lo:hi]
             + convw_ref[2:3, lo:hi] * u)
        ubuf[0:SUBLANES, lo:hi] = ubuf[tm:tm + SUBLANES, lo:hi]
        cbuf[:, lo:hi] = b_gate * y

    conv_steps = []
    for lo in range(0, c3, CONV_CHUNK):
        parts = []
        for base in (0, c3, 2 * c3):
            conv_steps.append(lambda lo=lo, base=base, parts=parts: parts.append(
                in_proj(base + lo, base + lo + CONV_CHUNK)))
        conv_steps.append(lambda lo=lo, parts=parts: conv_finish(lo, lo + CONV_CHUNK, *parts))
    out_c = []
    conv_steps.append(lambda: out_c.append(
        x_ref[0] + jnp.dot(_rms(cbuf[...], cog_ref[...]).astype(BF16), wo_ref[0:c3, :], preferred_element_type=F32)))

    kv = in_proj(KV_OFF + ATTN_WIDTH, IN_COLS)
    q_t = lax.dot_general(wqt[...], h, (((1,), (1,)), ((), ())), preferred_element_type=F32)
    lane = lax.broadcasted_iota(jnp.int32, (1, LANES), 1)
    lo_half = lane < HEAD_DIM

    k_sq = kv[:, 0:LANES] * kv[:, 0:LANES]
    ms_lo = jnp.sum(jnp.where(lo_half, k_sq, 0.0), axis=-1, keepdims=True)
    ms_hi = jnp.sum(jnp.where(lo_half, 0.0, k_sq), axis=-1, keepdims=True)
    kn = kv[:, 0:LANES] * lax.rsqrt(jnp.where(lo_half, ms_lo, ms_hi) * (1.0 / HEAD_DIM) + EPS) * kg_ref[...]
    k_a0 = jnp.where(lo_half, kn, 0.0)
    k_b1 = jnp.where(lo_half, 0.0, kn)
    kk[0, BLOCK:BLOCK + tm, :] = k_a0.astype(BF16)
    kk[1, BLOCK:BLOCK + tm, :] = pltpu.roll(k_a0, HEAD_DIM, 1).astype(BF16)
    kk[2, BLOCK:BLOCK + tm, :] = pltpu.roll(k_b1, HEAD_DIM, 1).astype(BF16)
    kk[3, BLOCK:BLOCK + tm, :] = k_b1.astype(BF16)

    for hd in range(N_Q_HEADS):
        c, hf = divmod(hd, 2)
        g, r = divmod(c, COLS_PER_KV)
        t = q_t[hd * HEAD_DIM:(hd + 1) * HEAD_DIM, :]
        inv = lax.rsqrt(jnp.mean(t * t, axis=0, keepdims=True) + EPS)
        for b in range(nblk):
            blk = slice(b * BLOCK, (b + 1) * BLOCK)
            qq[g, b, hf * HEAD_DIM:(hf + 1) * HEAD_DIM, r * BLOCK:(r + 1) * BLOCK] = (
                t[:, blk] * inv[:, blk] * qgt_ref[...]).astype(BF16)

    v_t = kv[:, LANES:2 * LANES].T
    ones_row = (lax.broadcasted_iota(jnp.int32, (HEAD_DIM, tm), 0) == 0).astype(F32)
    for g in range(N_KV_HEADS):
        v_g = v_t[g * HEAD_DIM:(g + 1) * HEAD_DIM, :]
        vt[2 * g, :, BLOCK:BLOCK + tm] = jnp.concatenate([v_g, ones_row], axis=0).astype(BF16)
        vt[2 * g + 1, :, BLOCK:BLOCK + tm] = jnp.concatenate([ones_row, v_g], axis=0).astype(BF16)

    pairs = [(b, g) for b in range(nblk) for g in range(N_KV_HEADS)]

    key_i = lax.broadcasted_iota(jnp.int32, (2 * BLOCK, COLS_PER_KV * BLOCK), 0)
    qry_i = lax.broadcasted_iota(jnp.int32, (2 * BLOCK, COLS_PER_KV * BLOCK), 1) & (BLOCK - 1)
    band = (key_i > qry_i) & (key_i <= qry_i + BLOCK)
    first_band = band & (key_i >= jnp.where(j == 0, BLOCK, 0))
    cap = jnp.where(band, jnp.inf, NEG_INF)
    first_cap = jnp.where(first_band, jnp.inf, NEG_INF)

    def scores(i):
        b, g = pairs[i]
        for half in range(2):
            keys = kk[2 * g + half, b * BLOCK:(b + 2) * BLOCK, :]
            sbuf[2 * i + half] = jnp.dot(keys, qq[g, b], preferred_element_type=F32)

    sink_term = {}

    def softmax(i):
        b, g = pairs[i]
        for half in range(2):
            s = jnp.minimum(sbuf[2 * i + half], first_cap if b == 0 else cap)
            sink = sink_ref[2 * g + half:2 * g + half + 1, :]
            m = jnp.maximum(jnp.max(s, axis=0, keepdims=True), sink)
            sink_term[2 * i + half] = jnp.exp2(sink - m)
            pbuf[2 * i + half] = jnp.exp2(s - m).astype(BF16)

    def weighted_values(i):
        b, g = pairs[i]
        keys = slice(b * BLOCK, (b + 2) * BLOCK)
        o_lo = jnp.dot(vt[2 * g, :, keys], pbuf[2 * i], preferred_element_type=F32)
        o_hi = jnp.dot(vt[2 * g + 1, :, keys], pbuf[2 * i + 1], preferred_element_type=F32)
        inv_lo = 1.0 / (o_lo[HEAD_DIM:HEAD_DIM + 1, :] + sink_term[2 * i])
        inv_hi = 1.0 / (o_hi[0:1, :] + sink_term[2 * i + 1])
        o_t = jnp.concatenate([o_lo[0:HEAD_DIM, :] * inv_lo, o_hi[HEAD_DIM:2 * HEAD_DIM, :] * inv_hi], axis=0)
        for r in range(COLS_PER_KV):
            c = COLS_PER_KV * g + r
            abuf[b * BLOCK:(b + 1) * BLOCK, c * LANES:(c + 1) * LANES] = o_t[:, r * BLOCK:(r + 1) * BLOCK].T

    n_iter, n_conv = len(pairs) + 2, len(conv_steps)
    for i in range(n_iter):
        while len(conv_steps) * n_iter > (n_iter - 1 - i) * n_conv:
            conv_steps.pop(0)()
        if i < len(pairs):
            scores(i)
        if 0 <= i - 1 < len(pairs):
            softmax(i - 1)
        if i - 2 >= 0:
            weighted_values(i - 2)

    kk[:, 0:BLOCK, :] = kk[:, tm:tm + BLOCK, :]
    vt[:, :, 0:BLOCK] = vt[:, :, tm:tm + BLOCK]

    mix_a = _rms(abuf[...], aog_ref[...]).astype(BF16)
    out_ref[0] = out_c[0] + jnp.dot(mix_a, wo_ref[c3:c3 + ATTN_WIDTH, :], preferred_element_type=F32)


def _ffn_kernel(x_ref, g_ref, wg_ref, wu_ref, wd_ref, out_ref):
    n_sub = x_ref.shape[0] // FF_ROWS
    chunks = [(lo, min(lo + FF_CHUNK, D_FF)) for lo in range(0, D_FF, FF_CHUNK)]
    items = [(r, c) for r in range(n_sub) for c in range(len(chunks))]

    def rows(r):
        return slice(r * FF_ROWS, (r + 1) * FF_ROWS)

    def normed(r):
        x = x_ref[rows(r), :]
        inv_rms = lax.rsqrt(jnp.mean(x * x, axis=-1, keepdims=True) + EPS)
        return (x * g_ref[...]).astype(BF16), inv_rms

    def gate_up(h, c):
        lo, hi = chunks[c]
        xg, inv_rms = h
        return (jnp.dot(xg, wg_ref[:, lo:hi], preferred_element_type=F32) * inv_rms,
                jnp.dot(xg, wu_ref[:, lo:hi], preferred_element_type=F32) * inv_rms)

    h = {0: normed(0)}
    nxt = gate_up(h[0], 0)
    acc = None
    for k, (r, c) in enumerate(items):
        gate, up = nxt
        if c == 0:
            acc = x_ref[rows(r), :]
            if r + 1 < n_sub:
                h[r + 1] = normed(r + 1)
        if k + 1 < len(items):
            r1, c1 = items[k + 1]
            nxt = gate_up(h[r1], c1)
        act = (gate * jax.nn.sigmoid(gate) * up).astype(BF16)
        lo, hi = chunks[c]
        acc = acc + jnp.dot(act, wd_ref[lo:hi, :], preferred_element_type=F32)
        if c == len(chunks) - 1:
            out_ref[rows(r), :] = acc


def _resident(shape, index_map):
    return pl.BlockSpec(shape, index_map, pipeline_mode=pl.Buffered(1))


def _mixer(x, g1, w_in, conv_w, qg_t, kg2, sink_rows, cog, aog, w_o, to_cast, layer, tm):
    b, seq, d = x.shape
    steps = b * (seq // tm)
    lsel3 = lambda bi, ji: (layer, 0, 0)
    top2 = lambda bi, ji: (0, 0)
    vec = lambda n: _resident((None, 1, n), lsel3)
    n_units = (tm // BLOCK) * N_KV_HEADS * 2
    cast_in, cast_out, cast_shapes = [], [], []
    for w, wl in to_cast:
        rows, cols = w.shape[1] // steps, w.shape[2]
        assert rows * steps == w.shape[1] and rows % (2 * SUBLANES) == 0
        cast_in.append(pl.BlockSpec((None, rows, cols), lambda bi, ji, wl=wl: (wl, bi * (seq // tm) + ji, 0)))
        cast_out.append(pl.BlockSpec((rows, cols), lambda bi, ji: (bi * (seq // tm) + ji, 0)))
        cast_shapes.append(jax.ShapeDtypeStruct(w.shape[1:], BF16))
    outs = pl.pallas_call(
        functools.partial(_mixer_kernel, tm=tm, n_cast=len(to_cast)),
        name=f"mixer_l{layer}",
        grid=(b, seq // tm),
        in_specs=[
            pl.BlockSpec((1, tm, d), lambda bi, ji: (bi, ji, 0)),
            vec(d),
            _resident((d, IN_COLS), top2),
            _resident((None, SUBLANES, CONV_CHANNELS), lsel3),
            _resident((None, HEAD_DIM, LANES), lsel3),
            vec(LANES),
            _resident((None, SUBLANES, COLS_PER_KV * BLOCK), lsel3),
            vec(CONV_CHANNELS),
            vec(ATTN_WIDTH),
            _resident((CONV_CHANNELS + ATTN_WIDTH, d), top2),
        ] + cast_in,
        out_specs=[pl.BlockSpec((1, tm, d), lambda bi, ji: (bi, ji, 0))] + cast_out,
        out_shape=[jax.ShapeDtypeStruct(x.shape, x.dtype)] + cast_shapes,
        scratch_shapes=[
            pltpu.VMEM((tm + SUBLANES, CONV_CHANNELS), F32),
            pltpu.VMEM((N_KV_HEADS, tm // BLOCK, LANES, COLS_PER_KV * BLOCK), BF16),
            pltpu.VMEM((4, tm + BLOCK, LANES), BF16),
            pltpu.VMEM((4, LANES, tm + BLOCK), BF16),
            pltpu.VMEM((n_units, 2 * BLOCK, COLS_PER_KV * BLOCK), F32),
            pltpu.VMEM((n_units, 2 * BLOCK, COLS_PER_KV * BLOCK), BF16),
            pltpu.VMEM((tm, ATTN_WIDTH), F32),
            pltpu.VMEM((tm, CONV_CHANNELS), F32),
            pltpu.VMEM((ATTN_WIDTH, d), BF16),
        ],
        compiler_params=pltpu.CompilerParams(
            dimension_semantics=("arbitrary", "arbitrary"),
            vmem_limit_bytes=VMEM_LIMIT_BYTES,
        ),
    )(x, g1, w_in, conv_w, qg_t, kg2, sink_rows, cog, aog, w_o, *[w for w, _ in to_cast])
    return outs[0], outs[1:]


def _ffn(x2d, g, w_gate, w_up, w_down, layer, tm):
    n, d = x2d.shape
    lsel3 = lambda i: (layer, 0, 0)
    top2 = lambda i: (0, 0)
    return pl.pallas_call(
        _ffn_kernel,
        name=f"ffn_l{layer}",
        grid=(n // tm,),
        in_specs=[
            pl.BlockSpec((tm, d), lambda i: (i, 0)),
            _resident((None, 1, d), lsel3),
            _resident((d, D_FF), top2),
            _resident((d, D_FF), top2),
            _resident((D_FF, d), top2),
        ],
        out_specs=pl.BlockSpec((tm, d), lambda i: (i, 0)),
        out_shape=jax.ShapeDtypeStruct(x2d.shape, x2d.dtype),
        compiler_params=pltpu.CompilerParams(
            dimension_semantics=("arbitrary",),
            vmem_limit_bytes=VMEM_LIMIT_BYTES,
        ),
    )(x2d, g, w_gate, w_up, w_down)


def kernel(x, norm1_g, w_in, conv_w, q_norm_g, k_norm_g, sinks, conv_out_g, attn_out_g, w_o, norm2_g,
           w_gate, w_up, w_down):
    b, seq, d = x.shape
    depth = w_in.shape[0]
    assert seq % TM_MIXER == 0 and (b * seq) % TM_FFN == 0 and TM_MIXER % BLOCK == 0

    row3 = lambda a: a.reshape(depth, 1, a.shape[-1])
    qg_t = jnp.broadcast_to((q_norm_g * (HEAD_DIM ** -0.5 * LOG2E))[:, :, None], (depth, HEAD_DIM, LANES))
    sink_rows = jnp.repeat((sinks * LOG2E).reshape(depth, N_KV_HEADS, COLS_PER_KV, 2).transpose(0, 1, 3, 2)
                           .reshape(depth, 2 * N_KV_HEADS, COLS_PER_KV), BLOCK, axis=-1)
    sink_rows = jnp.pad(sink_rows, ((0, 0), (0, SUBLANES - 2 * N_KV_HEADS), (0, 0)))
    kg2 = row3(jnp.tile(k_norm_g, (1, 2)))
    conv_w8 = jnp.pad(conv_w, ((0, 0), (0, SUBLANES - CONV_WIDTH), (0, 0)))
    w_in_b, w_o_b = w_in[0].astype(BF16), w_o[0].astype(BF16)
    g1, g2 = row3(norm1_g), row3(norm2_g)
    cog, aog = row3(conv_out_g), row3(attn_out_g)

    for l in range(depth):
        to_cast = [(w_gate, l), (w_up, l), (w_down, l)]
        if l + 1 < depth:
            to_cast += [(w_in, l + 1), (w_o, l + 1)]
        x, cast = _mixer(x, g1, w_in_b, conv_w8, qg_t, kg2, sink_rows, cog, aog, w_o_b, to_cast, l, TM_MIXER)
        w_gate_b, w_up_b, w_down_b = cast[:3]
        if l + 1 < depth:
            w_in_b, w_o_b = cast[3:]
        x = _ffn(x.reshape(b * seq, d), g2, w_gate_b, w_up_b, w_down_b, l, TM_FFN).reshape(b, seq, d)
    return x
```

```python
import functools

import jax
import jax.numpy as jnp
from jax import lax
from jax.experimental import pallas as pl
from jax.experimental.pallas import tpu as pltpu

D_MODEL = 1024
CONV_CHANNELS = 512
CONV_WIDTH = 3
N_Q_HEADS = 8
N_KV_HEADS = 2
HEAD_DIM = 64
ATTN_WIDTH = N_Q_HEADS * HEAD_DIM
BLOCK = 128
IN_COLS = 3 * CONV_CHANNELS + (N_Q_HEADS + 2 * N_KV_HEADS) * HEAD_DIM
D_FF = 2816
EPS = 1e-6
NEG_INF = -1e30

LANES = 128
SUBLANES = 8
Q_COLS = ATTN_WIDTH // LANES
COLS_PER_KV = Q_COLS // N_KV_HEADS
LOG2E = 1.4426950408889634
KV_OFF = 3 * CONV_CHANNELS
VMEM_LIMIT_BYTES = 56 * 1024 * 1024

TM_MIXER = 1024
CONV_CHUNK = 256
N_MIXER_IN = 10
TM_FFN = 1024
FF_ROWS = 512
FF_CHUNK = 1536

F32 = jnp.float32
BF16 = jnp.bfloat16


def _rms(x, g):
    ms = jnp.mean(x * x, axis=-1, keepdims=True)
    return x * lax.rsqrt(ms + EPS) * g


def _mixer_kernel(*refs, tm, n_cast, layer):
    (x_ref, g1_ref, win_ref, convw_ref, qgt_ref, kg_ref, sink_ref, cog_ref, aog_ref, wo_ref) = refs[:N_MIXER_IN]
    cast_src = refs[N_MIXER_IN:N_MIXER_IN + n_cast]
    out_ref = refs[N_MIXER_IN + n_cast]
    cast_dst = refs[N_MIXER_IN + n_cast + 1:N_MIXER_IN + 2 * n_cast + 1]
    ubuf, qq, kk, vt, sbuf, pbuf, abuf, cbuf, wqt = refs[N_MIXER_IN + 2 * n_cast + 1:]
    j = pl.program_id(1)
    this_layer = slice(layer, layer + 1)

    for src, dst in zip(cast_src, cast_dst):
        dst[...] = src[...].astype(BF16)
    c3 = CONV_CHANNELS
    nblk = tm // BLOCK

    @pl.when((pl.program_id(0) == 0) & (j == 0))
    def _():
        wqt[...] = win_ref[:, KV_OFF:KV_OFF + ATTN_WIDTH].astype(F32).T.astype(BF16)

    @pl.when(j == 0)
    def _():
        ubuf[0:SUBLANES, :] = jnp.zeros((SUBLANES, c3), F32)
        kk[:, 0:BLOCK, :] = jnp.zeros((4, BLOCK, LANES), BF16)
        vt[:, :, 0:BLOCK] = jnp.zeros((4, LANES, BLOCK), BF16)

    h = _rms(x_ref[0], g1_ref[this_layer, :]).astype(BF16)

    def in_proj(lo, hi):
        return jnp.dot(h, win_ref[:, lo:hi], preferred_element_type=F32)

    def conv_finish(lo, hi, b_gate, c_gate, hc):
        u = c_gate * hc
        ubuf[SUBLANES:SUBLANES + tm, lo:hi] = u
        y = (convw_ref[0:1, lo:hi] * ubuf[SUBLANES - 2:SUBLANES - 2 + tm, lo:hi]
             + convw_ref[1:2, lo:hi] * ubuf[SUBLANES - 1:SUBLANES - 1 + tm, lo:hi]
             + convw_ref[2:3, lo:hi] * u)
        ubuf[0:SUBLANES, lo:hi] = ubuf[tm:tm + SUBLANES, lo:hi]
        cbuf[:, lo:hi] = b_gate * y

    conv_steps = []
    for lo in range(0, c3, CONV_CHUNK):
        parts = []
        for base in (0, c3, 2 * c3):
            conv_steps.append(lambda lo=lo, base=base, parts=parts: parts.append(
                in_proj(base + lo, base + lo + CONV_CHUNK)))
        conv_steps.append(lambda lo=lo, parts=parts: conv_finish(lo, lo + CONV_CHUNK, *parts))
    out_c = []
    conv_steps.append(lambda: out_c.append(
        x_ref[0] + jnp.dot(_rms(cbuf[...], cog_ref[this_layer, :]).astype(BF16), wo_ref[0:c3, :], preferred_element_type=F32)))

    kv = in_proj(KV_OFF + ATTN_WIDTH, IN_COLS)
    q_t = lax.dot_general(wqt[...], h, (((1,), (1,)), ((), ())), preferred_element_type=F32)
    lane = lax.broadcasted_iota(jnp.int32, (1, LANES), 1)
    lo_half = lane < HEAD_DIM

    k_sq = kv[:, 0:LANES] * kv[:, 0:LANES]
    ms_lo = jnp.sum(jnp.where(lo_half, k_sq, 0.0), axis=-1, keepdims=True)
    ms_hi = jnp.sum(jnp.where(lo_half, 0.0, k_sq), axis=-1, keepdims=True)
    kn = kv[:, 0:LANES] * lax.rsqrt(jnp.where(lo_half, ms_lo, ms_hi) * (1.0 / HEAD_DIM) + EPS) * kg_ref[this_layer, :]
    k_a0 = jnp.where(lo_half, kn, 0.0)
    k_b1 = jnp.where(lo_half, 0.0, kn)
    kk[0, BLOCK:BLOCK + tm, :] = k_a0.astype(BF16)
    kk[1, BLOCK:BLOCK + tm, :] = pltpu.roll(k_a0, HEAD_DIM, 1).astype(BF16)
    kk[2, BLOCK:BLOCK + tm, :] = pltpu.roll(k_b1, HEAD_DIM, 1).astype(BF16)
    kk[3, BLOCK:BLOCK + tm, :] = k_b1.astype(BF16)

    for hd in range(N_Q_HEADS):
        c, hf = divmod(hd, 2)
        g, r = divmod(c, COLS_PER_KV)
        t = q_t[hd * HEAD_DIM:(hd + 1) * HEAD_DIM, :]
        inv = lax.rsqrt(jnp.mean(t * t, axis=0, keepdims=True) + EPS)
        for b in range(nblk):
            blk = slice(b * BLOCK, (b + 1) * BLOCK)
            qq[g, b, hf * HEAD_DIM:(hf + 1) * HEAD_DIM, r * BLOCK:(r + 1) * BLOCK] = (
                t[:, blk] * inv[:, blk] * qgt_ref[...]).astype(BF16)

    v_t = kv[:, LANES:2 * LANES].T
    ones_row = (lax.broadcasted_iota(jnp.int32, (HEAD_DIM, tm), 0) == 0).astype(F32)
    for g in range(N_KV_HEADS):
        v_g = v_t[g * HEAD_DIM:(g + 1) * HEAD_DIM, :]
        vt[2 * g, :, BLOCK:BLOCK + tm] = jnp.concatenate([v_g, ones_row], axis=0).astype(BF16)
        vt[2 * g + 1, :, BLOCK:BLOCK + tm] = jnp.concatenate([ones_row, v_g], axis=0).astype(BF16)

    pairs = [(b, g) for b in range(nblk) for g in range(N_KV_HEADS)]

    key_i = lax.broadcasted_iota(jnp.int32, (2 * BLOCK, COLS_PER_KV * BLOCK), 0)
    qry_i = lax.broadcasted_iota(jnp.int32, (2 * BLOCK, COLS_PER_KV * BLOCK), 1) & (BLOCK - 1)
    band = (key_i > qry_i) & (key_i <= qry_i + BLOCK)
    first_band = band & (key_i >= jnp.where(j == 0, BLOCK, 0))
    cap = jnp.where(band, jnp.inf, NEG_INF)
    first_cap = jnp.where(first_band, jnp.inf, NEG_INF)

    def scores(i):
        b, g = pairs[i]
        for half in range(2):
            keys = kk[2 * g + half, b * BLOCK:(b + 2) * BLOCK, :]
            sbuf[2 * i + half] = jnp.dot(keys, qq[g, b], preferred_element_type=F32)

    sink_term = {}

    def softmax(i):
        b, g = pairs[i]
        for half in range(2):
            s = jnp.minimum(sbuf[2 * i + half], first_cap if b == 0 else cap)
            sink = sink_ref[2 * g + half:2 * g + half + 1, :]
            m = jnp.maximum(jnp.max(s, axis=0, keepdims=True), sink)
            sink_term[2 * i + half] = jnp.exp2(sink - m)
            pbuf[2 * i + half] = jnp.exp2(s - m).astype(BF16)

    def weighted_values(i):
        b, g = pairs[i]
        keys = slice(b * BLOCK, (b + 2) * BLOCK)
        o_lo = jnp.dot(vt[2 * g, :, keys], pbuf[2 * i], preferred_element_type=F32)
        o_hi = jnp.dot(vt[2 * g + 1, :, keys], pbuf[2 * i + 1], preferred_element_type=F32)
        inv_lo = 1.0 / (o_lo[HEAD_DIM:HEAD_DIM + 1, :] + sink_term[2 * i])
        inv_hi = 1.0 / (o_hi[0:1, :] + sink_term[2 * i + 1])
        o_t = jnp.concatenate([o_lo[0:HEAD_DIM, :] * inv_lo, o_hi[HEAD_DIM:2 * HEAD_DIM, :] * inv_hi], axis=0)
        for r in range(COLS_PER_KV):
            c = COLS_PER_KV * g + r
            abuf[b * BLOCK:(b + 1) * BLOCK, c * LANES:(c + 1) * LANES] = o_t[:, r * BLOCK:(r + 1) * BLOCK].T

    n_iter, n_conv = len(pairs) + 2, len(conv_steps)
    for i in range(n_iter):
        while len(conv_steps) * n_iter > (n_iter - 1 - i) * n_conv:
            conv_steps.pop(0)()
        if i < len(pairs):
            scores(i)
        if 0 <= i - 1 < len(pairs):
            softmax(i - 1)
        if i - 2 >= 0:
            weighted_values(i - 2)

    kk[:, 0:BLOCK, :] = kk[:, tm:tm + BLOCK, :]
    vt[:, :, 0:BLOCK] = vt[:, :, tm:tm + BLOCK]

    mix_a = _rms(abuf[...], aog_ref[this_layer, :]).astype(BF16)
    out_ref[0] = out_c[0] + jnp.dot(mix_a, wo_ref[c3:c3 + ATTN_WIDTH, :], preferred_element_type=F32)


def _ffn_kernel(x_ref, g_ref, wg_ref, wu_ref, wd_ref, out_ref, *, layer):
    n_sub = x_ref.shape[0] // FF_ROWS
    chunks = [(lo, min(lo + FF_CHUNK, D_FF)) for lo in range(0, D_FF, FF_CHUNK)]
    items = [(r, c) for r in range(n_sub) for c in range(len(chunks))]

    def rows(r):
        return slice(r * FF_ROWS, (r + 1) * FF_ROWS)

    def normed(r):
        x = x_ref[rows(r), :]
        inv_rms = lax.rsqrt(jnp.mean(x * x, axis=-1, keepdims=True) + EPS)
        return (x * g_ref[layer:layer + 1, :]).astype(BF16), inv_rms

    def gate_up(h, c):
        lo, hi = chunks[c]
        xg, inv_rms = h
        return (jnp.dot(xg, wg_ref[:, lo:hi], preferred_element_type=F32) * inv_rms,
                jnp.dot(xg, wu_ref[:, lo:hi], preferred_element_type=F32) * inv_rms)

    h = {0: normed(0)}
    nxt = gate_up(h[0], 0)
    acc = None
    for k, (r, c) in enumerate(items):
        gate, up = nxt
        if c == 0:
            acc = x_ref[rows(r), :]
            if r + 1 < n_sub:
                h[r + 1] = normed(r + 1)
        if k + 1 < len(items):
            r1, c1 = items[k + 1]
            nxt = gate_up(h[r1], c1)
        act = (gate * jax.nn.sigmoid(gate) * up).astype(BF16)
        lo, hi = chunks[c]
        acc = acc + jnp.dot(act, wd_ref[lo:hi, :], preferred_element_type=F32)
        if c == len(chunks) - 1:
            out_ref[rows(r), :] = acc


def _resident(shape, index_map):
    return pl.BlockSpec(shape, index_map, pipeline_mode=pl.Buffered(1))


def _mixer(x, g1, w_in, conv_w, qg_t, kg2, sink_rows, cog, aog, w_o, to_cast, layer, tm):
    b, seq, d = x.shape
    depth = g1.shape[0]
    nj = seq // tm
    steps = b * nj
    lsel3 = lambda bi, ji: (layer, 0, 0)
    top2 = lambda bi, ji: (0, 0)
    vec = lambda n: _resident((depth, n), top2)
    n_units = (tm // BLOCK) * N_KV_HEADS * 2
    cast_in, cast_out, cast_shapes = [], [], []
    for w, wl in to_cast:
        rows, cols = w.shape[1] // steps, w.shape[2]
        assert rows * steps == w.shape[1] and rows % (2 * SUBLANES) == 0
        cast_in.append(pl.BlockSpec((None, rows, cols), lambda bi, ji, wl=wl: (wl, bi * nj + ji, 0)))
        cast_out.append(pl.BlockSpec((rows, cols), lambda bi, ji: (bi * nj + ji, 0)))
        cast_shapes.append(jax.ShapeDtypeStruct(w.shape[1:], BF16))
    outs = pl.pallas_call(
        functools.partial(_mixer_kernel, tm=tm, n_cast=len(to_cast), layer=layer),
        name=f"mixer_l{layer}",
        grid=(b, nj),
        in_specs=[
            pl.BlockSpec((1, tm, d), lambda bi, ji: (bi, ji, 0)),
            vec(d),
            _resident((d, IN_COLS), top2),
            _resident((None, SUBLANES, CONV_CHANNELS), lsel3),
            _resident((None, HEAD_DIM, LANES), lsel3),
            vec(LANES),
            _resident((None, SUBLANES, COLS_PER_KV * BLOCK), lsel3),
            vec(CONV_CHANNELS),
            vec(ATTN_WIDTH),
            _resident((CONV_CHANNELS + ATTN_WIDTH, d), top2),
        ] + cast_in,
        out_specs=[pl.BlockSpec((1, tm, d), lambda bi, ji: (bi, ji, 0))] + cast_out,
        out_shape=[jax.ShapeDtypeStruct(x.shape, x.dtype)] + cast_shapes,
        scratch_shapes=[
            pltpu.VMEM((tm + SUBLANES, CONV_CHANNELS), F32),
            pltpu.VMEM((N_KV_HEADS, tm // BLOCK, LANES, COLS_PER_KV * BLOCK), BF16),
            pltpu.VMEM((4, tm + BLOCK, LANES), BF16),
            pltpu.VMEM((4, LANES, tm + BLOCK), BF16),
            pltpu.VMEM((n_units, 2 * BLOCK, COLS_PER_KV * BLOCK), F32),
            pltpu.VMEM((n_units, 2 * BLOCK, COLS_PER_KV * BLOCK), BF16),
            pltpu.VMEM((tm, ATTN_WIDTH), F32),
            pltpu.VMEM((tm, CONV_CHANNELS), F32),
            pltpu.VMEM((ATTN_WIDTH, d), BF16),
        ],
        compiler_params=pltpu.CompilerParams(
            dimension_semantics=("arbitrary", "arbitrary"),
            vmem_limit_bytes=VMEM_LIMIT_BYTES,
        ),
    )(x, g1, w_in, conv_w, qg_t, kg2, sink_rows, cog, aog, w_o, *[w for w, _ in to_cast])
    return outs[0], outs[1:]


def _ffn(x2d, g, w_gate, w_up, w_down, layer, tm):
    n, d = x2d.shape
    top2 = lambda i: (0, 0)
    return pl.pallas_call(
        functools.partial(_ffn_kernel, layer=layer),
        name=f"ffn_l{layer}",
        grid=(n // tm,),
        in_specs=[
            pl.BlockSpec((tm, d), lambda i: (i, 0)),
            _resident(g.shape, top2),
            _resident((d, D_FF), top2),
            _resident((d, D_FF), top2),
            _resident((D_FF, d), top2),
        ],
        out_specs=pl.BlockSpec((tm, d), lambda i: (i, 0)),
        out_shape=jax.ShapeDtypeStruct(x2d.shape, x2d.dtype),
        compiler_params=pltpu.CompilerParams(
            dimension_semantics=("arbitrary",),
            vmem_limit_bytes=VMEM_LIMIT_BYTES,
        ),
    )(x2d, g, w_gate, w_up, w_down)


def kernel(x, norm1_g, w_in, conv_w, q_norm_g, k_norm_g, sinks, conv_out_g, attn_out_g, w_o, norm2_g,
           w_gate, w_up, w_down):
    b, seq, d = x.shape
    depth = w_in.shape[0]
    assert seq % TM_MIXER == 0 and (b * seq) % TM_FFN == 0 and TM_MIXER % BLOCK == 0

    qg_t = jnp.broadcast_to((q_norm_g * (HEAD_DIM ** -0.5 * LOG2E))[:, :, None], (depth, HEAD_DIM, LANES))
    sink_rows = jnp.repeat((sinks * LOG2E).reshape(depth, N_KV_HEADS, COLS_PER_KV, 2).transpose(0, 1, 3, 2)
                           .reshape(depth, 2 * N_KV_HEADS, COLS_PER_KV), BLOCK, axis=-1)
    sink_rows = jnp.pad(sink_rows, ((0, 0), (0, SUBLANES - 2 * N_KV_HEADS), (0, 0)))
    kg2 = jnp.tile(k_norm_g, (1, 2))
    conv_w8 = jnp.pad(conv_w, ((0, 0), (0, SUBLANES - CONV_WIDTH), (0, 0)))
    w_in_b, w_o_b = w_in[0].astype(BF16), w_o[0].astype(BF16)

    for l in range(depth):
        to_cast = [(w_gate, l), (w_up, l), (w_down, l)]
        if l + 1 < depth:
            to_cast += [(w_in, l + 1), (w_o, l + 1)]
        x, cast = _mixer(x, norm1_g, w_in_b, conv_w8, qg_t, kg2, sink_rows, conv_out_g, attn_out_g, w_o_b, to_cast, l,
                         TM_MIXER)
        w_gate_b, w_up_b, w_down_b = cast[:3]
        if l + 1 < depth:
            w_in_b, w_o_b = cast[3:]
        x = _ffn(x.reshape(b * seq, d), norm2_g, w_gate_b, w_up_b, w_down_b, l, TM_FFN).reshape(b, seq, d)
    return x
```

```python
import functools

import jax
import jax.numpy as jnp
from jax import lax
from jax.experimental import pallas as pl
from jax.experimental.pallas import tpu as pltpu

D_MODEL = 1024
CONV_CHANNELS = 512
CONV_WIDTH = 3
N_Q_HEADS = 8
N_KV_HEADS = 2
HEAD_DIM = 64
ATTN_WIDTH = N_Q_HEADS * HEAD_DIM
BLOCK = 128
IN_COLS = 3 * CONV_CHANNELS + (N_Q_HEADS + 2 * N_KV_HEADS) * HEAD_DIM
D_FF = 2816
EPS = 1e-6
NEG_INF = -1e30

LANES = 128
SUBLANES = 8
Q_COLS = ATTN_WIDTH // LANES
COLS_PER_KV = Q_COLS // N_KV_HEADS
LOG2E = 1.4426950408889634
KV_OFF = 3 * CONV_CHANNELS
VMEM_LIMIT_BYTES = 60 * 1024 * 1024

TM_MIXER = 1024
CONV_CHUNK = 256
N_MIXER_IN = 9
TM_FFN = 2048
FF_ROWS = 512
FF_CHUNK = 1536

F32 = jnp.float32
BF16 = jnp.bfloat16


def _rms(x, g):
    ms = jnp.mean(x * x, axis=-1, keepdims=True)
    return x * lax.rsqrt(ms + EPS) * g


def _mixer_kernel(*refs, tm, n_cast, layer):
    (x_ref, g1_ref, win_ref, convw_ref, kg_ref, sink_ref, cog_ref, aog_ref, wo_ref) = refs[:N_MIXER_IN]
    cast_src = refs[N_MIXER_IN:N_MIXER_IN + n_cast]
    out_ref = refs[N_MIXER_IN + n_cast]
    cast_dst = refs[N_MIXER_IN + n_cast + 1:N_MIXER_IN + 2 * n_cast + 1]
    ubuf, qq, kk, vt, sbuf, pbuf, abuf, cbuf, wqt = refs[N_MIXER_IN + 2 * n_cast + 1:]
    j = pl.program_id(1)
    this_layer = slice(layer, layer + 1)

    for src, dst in zip(cast_src, cast_dst):
        dst[...] = src[...].astype(BF16)
    c3 = CONV_CHANNELS
    nblk = tm // BLOCK

    @pl.when((pl.program_id(0) == 0) & (j == 0))
    def _():
        wqt[...] = win_ref[:, KV_OFF:KV_OFF + ATTN_WIDTH].astype(F32).T.astype(BF16)

    @pl.when(j == 0)
    def _():
        ubuf[0:SUBLANES, :] = jnp.zeros((SUBLANES, c3), F32)
        kk[:, 0:BLOCK, :] = jnp.zeros((4, BLOCK, LANES), BF16)
        vt[:, :, 0:BLOCK] = jnp.zeros((4, LANES, BLOCK), BF16)

    h = _rms(x_ref[0], g1_ref[this_layer, :]).astype(BF16)

    def in_proj(lo, hi):
        return jnp.dot(h, win_ref[:, lo:hi], preferred_element_type=F32)

    def conv_finish(lo, hi, b_gate, c_gate, hc):
        u = c_gate * hc
        w0, w1, w2 = (convw_ref[t:t + 1, lo:hi] for t in range(CONV_WIDTH))
        y = w0 * pltpu.roll(u, 2, 0) + w1 * pltpu.roll(u, 1, 0) + w2 * u
        head = jnp.concatenate([ubuf[:, lo:hi], u[0:SUBLANES, :]], axis=0)
        y_head = (w0 * head[SUBLANES - 2:2 * SUBLANES - 2, :] + w1 * head[SUBLANES - 1:2 * SUBLANES - 1, :]
                  + w2 * u[0:SUBLANES, :])
        ubuf[:, lo:hi] = u[tm - SUBLANES:tm, :]
        cbuf[:, lo:hi] = b_gate * jnp.concatenate([y_head, y[SUBLANES:, :]], axis=0)

    conv_steps = []
    for lo in range(0, c3, CONV_CHUNK):
        parts = []
        for base in (0, c3, 2 * c3):
            conv_steps.append(lambda lo=lo, base=base, parts=parts: parts.append(
                in_proj(base + lo, base + lo + CONV_CHUNK)))
        conv_steps.append(lambda lo=lo, parts=parts: conv_finish(lo, lo + CONV_CHUNK, *parts))
    out_c = []
    conv_steps.append(lambda: out_c.append(
        x_ref[0] + jnp.dot(_rms(cbuf[...], cog_ref[this_layer, :]).astype(BF16), wo_ref[0:c3, :], preferred_element_type=F32)))

    kv = in_proj(KV_OFF + ATTN_WIDTH, IN_COLS)
    q_t = lax.dot_general(wqt[...], h, (((1,), (1,)), ((), ())), preferred_element_type=F32)
    lane = lax.broadcasted_iota(jnp.int32, (1, LANES), 1)
    lo_half = lane < HEAD_DIM

    k_sq = kv[:, 0:LANES] * kv[:, 0:LANES]
    ms_lo = jnp.sum(jnp.where(lo_half, k_sq, 0.0), axis=-1, keepdims=True)
    ms_hi = jnp.sum(jnp.where(lo_half, 0.0, k_sq), axis=-1, keepdims=True)
    kn = kv[:, 0:LANES] * lax.rsqrt(jnp.where(lo_half, ms_lo, ms_hi) * (1.0 / HEAD_DIM) + EPS) * kg_ref[this_layer, :]
    k_a0 = jnp.where(lo_half, kn, 0.0)
    k_b1 = jnp.where(lo_half, 0.0, kn)
    kk[0, BLOCK:BLOCK + tm, :] = k_a0.astype(BF16)
    kk[1, BLOCK:BLOCK + tm, :] = pltpu.roll(k_a0, HEAD_DIM, 1).astype(BF16)
    kk[2, BLOCK:BLOCK + tm, :] = pltpu.roll(k_b1, HEAD_DIM, 1).astype(BF16)
    kk[3, BLOCK:BLOCK + tm, :] = k_b1.astype(BF16)

    for hd in range(N_Q_HEADS):
        c, hf = divmod(hd, 2)
        g, r = divmod(c, COLS_PER_KV)
        t = q_t[hd * HEAD_DIM:(hd + 1) * HEAD_DIM, :]
        inv = lax.rsqrt(jnp.mean(t * t, axis=0, keepdims=True) + EPS)
        for b in range(nblk):
            blk = slice(b * BLOCK, (b + 1) * BLOCK)
            qq[g, b, hf * HEAD_DIM:(hf + 1) * HEAD_DIM, r * BLOCK:(r + 1) * BLOCK] = (
                t[:, blk] * inv[:, blk]).astype(BF16)

    v_t = kv[:, LANES:2 * LANES].T
    ones_row = (lax.broadcasted_iota(jnp.int32, (HEAD_DIM, tm), 0) == 0).astype(F32)
    for g in range(N_KV_HEADS):
        v_g = v_t[g * HEAD_DIM:(g + 1) * HEAD_DIM, :]
        vt[2 * g, :, BLOCK:BLOCK + tm] = jnp.concatenate([v_g, ones_row], axis=0).astype(BF16)
        vt[2 * g + 1, :, BLOCK:BLOCK + tm] = jnp.concatenate([ones_row, v_g], axis=0).astype(BF16)

    pairs = [(b, g) for b in range(nblk) for g in range(N_KV_HEADS)]

    key_i = lax.broadcasted_iota(jnp.int32, (2 * BLOCK, COLS_PER_KV * BLOCK), 0)
    qry_i = lax.broadcasted_iota(jnp.int32, (2 * BLOCK, COLS_PER_KV * BLOCK), 1) & (BLOCK - 1)
    band = (key_i > qry_i) & (key_i <= qry_i + BLOCK)
    first_band = band & (key_i >= jnp.where(j == 0, BLOCK, 0))
    cap = jnp.where(band, jnp.inf, NEG_INF)
    first_cap = jnp.where(first_band, jnp.inf, NEG_INF)

    def scores(i):
        b, g = pairs[i]
        for half in range(2):
            keys = kk[2 * g + half, b * BLOCK:(b + 2) * BLOCK, :]
            sbuf[2 * i + half] = jnp.dot(keys, qq[g, b], preferred_element_type=F32)

    sink_term = {}

    def softmax(i):
        b, g = pairs[i]
        for half in range(2):
            s = jnp.minimum(sbuf[2 * i + half], first_cap if b == 0 else cap)
            sink = sink_ref[2 * g + half:2 * g + half + 1, :]
            m = jnp.maximum(jnp.max(s, axis=0, keepdims=True), sink)
            sink_term[2 * i + half] = jnp.exp2(sink - m)
            pbuf[2 * i + half] = jnp.exp2(s - m).astype(BF16)

    def weighted_values(i):
        b, g = pairs[i]
        keys = slice(b * BLOCK, (b + 2) * BLOCK)
        o_lo = jnp.dot(vt[2 * g, :, keys], pbuf[2 * i], preferred_element_type=F32)
        o_hi = jnp.dot(vt[2 * g + 1, :, keys], pbuf[2 * i + 1], preferred_element_type=F32)
        inv_lo = 1.0 / (o_lo[HEAD_DIM:HEAD_DIM + 1, :] + sink_term[2 * i])
        inv_hi = 1.0 / (o_hi[0:1, :] + sink_term[2 * i + 1])
        o_t = jnp.concatenate([o_lo[0:HEAD_DIM, :] * inv_lo, o_hi[HEAD_DIM:2 * HEAD_DIM, :] * inv_hi], axis=0)
        for r in range(COLS_PER_KV):
            c = COLS_PER_KV * g + r
            abuf[b * BLOCK:(b + 1) * BLOCK, c * LANES:(c + 1) * LANES] = o_t[:, r * BLOCK:(r + 1) * BLOCK].T

    n_iter, n_conv = len(pairs) + 2, len(conv_steps)
    for i in range(n_iter):
        while len(conv_steps) * n_iter > (n_iter - 1 - i) * n_conv:
            conv_steps.pop(0)()
        if i < len(pairs):
            scores(i)
        if 0 <= i - 1 < len(pairs):
            softmax(i - 1)
        if i - 2 >= 0:
            weighted_values(i - 2)

    kk[:, 0:BLOCK, :] = kk[:, tm:tm + BLOCK, :]
    vt[:, :, 0:BLOCK] = vt[:, :, tm:tm + BLOCK]

    mix_a = _rms(abuf[...], aog_ref[this_layer, :]).astype(BF16)
    out_ref[0] = out_c[0] + jnp.dot(mix_a, wo_ref[c3:c3 + ATTN_WIDTH, :], preferred_element_type=F32)


def _ffn_kernel(x_ref, g_ref, wg_ref, wu_ref, wd_ref, out_ref, *, layer):
    n_sub = x_ref.shape[0] // FF_ROWS
    chunks = [(lo, min(lo + FF_CHUNK, D_FF)) for lo in range(0, D_FF, FF_CHUNK)]
    items = [(r, c) for r in range(n_sub) for c in range(len(chunks))]

    def rows(r):
        return slice(r * FF_ROWS, (r + 1) * FF_ROWS)

    def normed(r):
        x = x_ref[rows(r), :]
        inv_rms = lax.rsqrt(jnp.mean(x * x, axis=-1, keepdims=True) + EPS)
        return (x * g_ref[layer:layer + 1, :]).astype(BF16), inv_rms

    def gate_up(h, c):
        lo, hi = chunks[c]
        xg, inv_rms = h
        return (jnp.dot(xg, wg_ref[:, lo:hi], preferred_element_type=F32) * inv_rms,
                jnp.dot(xg, wu_ref[:, lo:hi], preferred_element_type=F32) * inv_rms)

    h = {0: normed(0)}
    nxt = gate_up(h[0], 0)
    acc = None
    for k, (r, c) in enumerate(items):
        gate, up = nxt
        if c == 0:
            acc = x_ref[rows(r), :]
            if r + 1 < n_sub:
                h[r + 1] = normed(r + 1)
        if k + 1 < len(items):
            r1, c1 = items[k + 1]
            nxt = gate_up(h[r1], c1)
        act = (gate * jax.nn.sigmoid(gate) * up).astype(BF16)
        lo, hi = chunks[c]
        acc = acc + jnp.dot(act, wd_ref[lo:hi, :], preferred_element_type=F32)
        if c == len(chunks) - 1:
            out_ref[rows(r), :] = acc


def _resident(shape, index_map):
    return pl.BlockSpec(shape, index_map, pipeline_mode=pl.Buffered(1))


def _mixer(x, g1, w_in, conv_w, kg2, sink_rows, cog, aog, w_o, to_cast, layer, tm):
    b, seq, d = x.shape
    depth = g1.shape[0]
    nj = seq // tm
    steps = b * nj
    lsel3 = lambda bi, ji: (layer, 0, 0)
    top2 = lambda bi, ji: (0, 0)
    vec = lambda n: _resident((depth, n), top2)
    n_units = (tm // BLOCK) * N_KV_HEADS * 2
    cast_in, cast_out, cast_shapes = [], [], []
    for w, wl in to_cast:
        rows, cols = w.shape[1] // steps, w.shape[2]
        assert rows * steps == w.shape[1] and rows % (2 * SUBLANES) == 0
        cast_in.append(pl.BlockSpec((None, rows, cols), lambda bi, ji, wl=wl: (wl, bi * nj + ji, 0)))
        cast_out.append(pl.BlockSpec((rows, cols), lambda bi, ji: (bi * nj + ji, 0)))
        cast_shapes.append(jax.ShapeDtypeStruct(w.shape[1:], BF16))
    outs = pl.pallas_call(
        functools.partial(_mixer_kernel, tm=tm, n_cast=len(to_cast), layer=layer),
        name=f"mixer_l{layer}",
        grid=(b, nj),
        in_specs=[
            pl.BlockSpec((1, tm, d), lambda bi, ji: (bi, ji, 0)),
            vec(d),
            _resident((d, IN_COLS), top2),
            _resident((None, SUBLANES, CONV_CHANNELS), lsel3),
            vec(LANES),
            _resident((None, SUBLANES, COLS_PER_KV * BLOCK), lsel3),
            vec(CONV_CHANNELS),
            vec(ATTN_WIDTH),
            _resident((CONV_CHANNELS + ATTN_WIDTH, d), top2),
        ] + cast_in,
        out_specs=[pl.BlockSpec((1, tm, d), lambda bi, ji: (bi, ji, 0))] + cast_out,
        out_shape=[jax.ShapeDtypeStruct(x.shape, x.dtype)] + cast_shapes,
        scratch_shapes=[
            pltpu.VMEM((SUBLANES, CONV_CHANNELS), F32),
            pltpu.VMEM((N_KV_HEADS, tm // BLOCK, LANES, COLS_PER_KV * BLOCK), BF16),
            pltpu.VMEM((4, tm + BLOCK, LANES), BF16),
            pltpu.VMEM((4, LANES, tm + BLOCK), BF16),
            pltpu.VMEM((n_units, 2 * BLOCK, COLS_PER_KV * BLOCK), F32),
            pltpu.VMEM((n_units, 2 * BLOCK, COLS_PER_KV * BLOCK), BF16),
            pltpu.VMEM((tm, ATTN_WIDTH), F32),
            pltpu.VMEM((tm, CONV_CHANNELS), F32),
            pltpu.VMEM((ATTN_WIDTH, d), BF16),
        ],
        compiler_params=pltpu.CompilerParams(
            dimension_semantics=("arbitrary", "arbitrary"),
            vmem_limit_bytes=VMEM_LIMIT_BYTES,
        ),
    )(x, g1, w_in, conv_w, kg2, sink_rows, cog, aog, w_o, *[w for w, _ in to_cast])
    return outs[0], outs[1:]


def _ffn(x2d, g, w_gate, w_up, w_down, layer, tm):
    n, d = x2d.shape
    top2 = lambda i: (0, 0)
    return pl.pallas_call(
        functools.partial(_ffn_kernel, layer=layer),
        name=f"ffn_l{layer}",
        grid=(n // tm,),
        in_specs=[
            pl.BlockSpec((tm, d), lambda i: (i, 0)),
            _resident(g.shape, top2),
            _resident((d, D_FF), top2),
            _resident((d, D_FF), top2),
            _resident((D_FF, d), top2),
        ],
        out_specs=pl.BlockSpec((tm, d), lambda i: (i, 0)),
        out_shape=jax.ShapeDtypeStruct(x2d.shape, x2d.dtype),
        compiler_params=pltpu.CompilerParams(
            dimension_semantics=("arbitrary",),
            vmem_limit_bytes=VMEM_LIMIT_BYTES,
        ),
    )(x2d, g, w_gate, w_up, w_down)


def kernel(x, norm1_g, w_in, conv_w, q_norm_g, k_norm_g, sinks, conv_out_g, attn_out_g, w_o, norm2_g,
           w_gate, w_up, w_down):
    b, seq, d = x.shape
    depth = w_in.shape[0]
    assert seq % TM_MIXER == 0 and (b * seq) % TM_FFN == 0 and TM_MIXER % BLOCK == 0

    sink_rows = jnp.repeat((sinks * LOG2E).reshape(depth, N_KV_HEADS, COLS_PER_KV, 2).transpose(0, 1, 3, 2)
                           .reshape(depth, 2 * N_KV_HEADS, COLS_PER_KV), BLOCK, axis=-1)
    sink_rows = jnp.pad(sink_rows, ((0, 0), (0, SUBLANES - 2 * N_KV_HEADS), (0, 0)))
    kg2 = jnp.tile(k_norm_g * q_norm_g * (HEAD_DIM ** -0.5 * LOG2E), (1, 2))
    conv_w8 = jnp.pad(conv_w, ((0, 0), (0, SUBLANES - CONV_WIDTH), (0, 0)))
    w_in_b, w_o_b = w_in[0].astype(BF16), w_o[0].astype(BF16)

    for l in range(depth):
        to_cast = [(w_gate, l), (w_up, l), (w_down, l)]
        if l + 1 < depth:
            to_cast += [(w_in, l + 1), (w_o, l + 1)]
        x, cast = _mixer(x, norm1_g, w_in_b, conv_w8, kg2, sink_rows, conv_out_g, attn_out_g, w_o_b, to_cast, l,
                         TM_MIXER)
        w_gate_b, w_up_b, w_down_b = cast[:3]
        if l + 1 < depth:
            w_in_b, w_o_b = cast[3:]
        x = _ffn(x.reshape(b * seq, d), norm2_g, w_gate_b, w_up_b, w_down_b, l, TM_FFN).reshape(b, seq, d)
    return x
```

```python
import functools

import jax
import jax.numpy as jnp
from jax import lax
from jax.experimental import pallas as pl
from jax.experimental.pallas import tpu as pltpu

D_MODEL = 1024
CONV_CHANNELS = 512
CONV_WIDTH = 3
N_Q_HEADS = 8
N_KV_HEADS = 2
HEAD_DIM = 64
ATTN_WIDTH = N_Q_HEADS * HEAD_DIM
BLOCK = 128
IN_COLS = 3 * CONV_CHANNELS + (N_Q_HEADS + 2 * N_KV_HEADS) * HEAD_DIM
D_FF = 2816
EPS = 1e-6
NEG_INF = -1e30

LANES = 128
SUBLANES = 8
Q_COLS = ATTN_WIDTH // LANES
COLS_PER_KV = Q_COLS // N_KV_HEADS
LOG2E = 1.4426950408889634
KV_OFF = 3 * CONV_CHANNELS
VMEM_LIMIT_BYTES = 56 * 1024 * 1024

TM_MIXER = 1024
CONV_CHUNK = 256
N_MIXER_IN = 9
TM_FFN = 1024
FF_ROWS = 512
FF_CHUNK = 1536

F32 = jnp.float32
BF16 = jnp.bfloat16


def _rms(x, g):
    ms = jnp.mean(x * x, axis=-1, keepdims=True)
    return x * lax.rsqrt(ms + EPS) * g


def _mixer_kernel(*refs, tm, n_cast, layer):
    (x_ref, g1_ref, win_ref, convw_ref, kg_ref, sink_ref, cog_ref, aog_ref, wo_ref) = refs[:N_MIXER_IN]
    cast_src = refs[N_MIXER_IN:N_MIXER_IN + n_cast]
    out_ref = refs[N_MIXER_IN + n_cast]
    cast_dst = refs[N_MIXER_IN + n_cast + 1:N_MIXER_IN + 2 * n_cast + 1]
    ubuf, qq, kk, vt, sbuf, pbuf, abuf, cbuf, wqt = refs[N_MIXER_IN + 2 * n_cast + 1:]
    j = pl.program_id(1)
    this_layer = slice(layer, layer + 1)

    for src, dst in zip(cast_src, cast_dst):
        dst[...] = src[...].astype(BF16)
    c3 = CONV_CHANNELS
    nblk = tm // BLOCK

    @pl.when((pl.program_id(0) == 0) & (j == 0))
    def _():
        wqt[...] = win_ref[:, KV_OFF:KV_OFF + ATTN_WIDTH].astype(F32).T.astype(BF16)

    @pl.when(j == 0)
    def _():
        ubuf[0:SUBLANES, :] = jnp.zeros((SUBLANES, c3), F32)
        kk[:, 0:BLOCK, :] = jnp.zeros((4, BLOCK, LANES), BF16)
        vt[:, :, 0:BLOCK] = jnp.zeros((4, LANES, BLOCK), BF16)

    x = x_ref[0]
    ms_x = jnp.mean(x * x, axis=-1, keepdims=True) + EPS
    r_x = lax.rsqrt(ms_x)
    h = (x * g1_ref[this_layer, :]).astype(BF16)

    def in_proj(lo, hi):
        return jnp.dot(h, win_ref[:, lo:hi], preferred_element_type=F32)

    def conv_finish(lo, hi, b_gate, c_gate, hc):
        u = c_gate * hc * (r_x * r_x)
        w0, w1, w2 = (convw_ref[t:t + 1, lo:hi] for t in range(CONV_WIDTH))
        y = w0 * pltpu.roll(u, 2, 0) + w1 * pltpu.roll(u, 1, 0) + w2 * u
        head = jnp.concatenate([ubuf[:, lo:hi], u[0:SUBLANES, :]], axis=0)
        y_head = (w0 * head[SUBLANES - 2:2 * SUBLANES - 2, :] + w1 * head[SUBLANES - 1:2 * SUBLANES - 1, :]
                  + w2 * u[0:SUBLANES, :])
        ubuf[:, lo:hi] = u[tm - SUBLANES:tm, :]
        cbuf[:, lo:hi] = b_gate * r_x * jnp.concatenate([y_head, y[SUBLANES:, :]], axis=0)

    conv_steps = []
    for lo in range(0, c3, CONV_CHUNK):
        parts = []
        for base in (0, c3, 2 * c3):
            conv_steps.append(lambda lo=lo, base=base, parts=parts: parts.append(
                in_proj(base + lo, base + lo + CONV_CHUNK)))
        conv_steps.append(lambda lo=lo, parts=parts: conv_finish(lo, lo + CONV_CHUNK, *parts))
    out_c = []
    conv_steps.append(lambda: out_c.append(
        x_ref[0] + jnp.dot(_rms(cbuf[...], cog_ref[this_layer, :]).astype(BF16), wo_ref[0:c3, :], preferred_element_type=F32)))

    kv = in_proj(KV_OFF + ATTN_WIDTH, IN_COLS)
    q_t = lax.dot_general(wqt[...], h, (((1,), (1,)), ((), ())), preferred_element_type=F32)
    lane = lax.broadcasted_iota(jnp.int32, (1, LANES), 1)
    lo_half = lane < HEAD_DIM

    k_sq = kv[:, 0:LANES] * kv[:, 0:LANES]
    ms_lo = jnp.sum(jnp.where(lo_half, k_sq, 0.0), axis=-1, keepdims=True)
    ms_hi = jnp.sum(jnp.where(lo_half, 0.0, k_sq), axis=-1, keepdims=True)
    kn = (kv[:, 0:LANES] * lax.rsqrt(jnp.where(lo_half, ms_lo, ms_hi) * (1.0 / HEAD_DIM) + EPS * ms_x)
          * kg_ref[this_layer, :])
    k_a0 = jnp.where(lo_half, kn, 0.0)
    k_b1 = jnp.where(lo_half, 0.0, kn)
    kk[0, BLOCK:BLOCK + tm, :] = k_a0.astype(BF16)
    kk[1, BLOCK:BLOCK + tm, :] = pltpu.roll(k_a0, HEAD_DIM, 1).astype(BF16)
    kk[2, BLOCK:BLOCK + tm, :] = pltpu.roll(k_b1, HEAD_DIM, 1).astype(BF16)
    kk[3, BLOCK:BLOCK + tm, :] = k_b1.astype(BF16)

    ms_x_row = ms_x.T
    for hd in range(N_Q_HEADS):
        c, hf = divmod(hd, 2)
        g, r = divmod(c, COLS_PER_KV)
        t = q_t[hd * HEAD_DIM:(hd + 1) * HEAD_DIM, :]
        inv = lax.rsqrt(jnp.mean(t * t, axis=0, keepdims=True) + EPS * ms_x_row)
        for b in range(nblk):
            blk = slice(b * BLOCK, (b + 1) * BLOCK)
            qq[g, b, hf * HEAD_DIM:(hf + 1) * HEAD_DIM, r * BLOCK:(r + 1) * BLOCK] = (
                t[:, blk] * inv[:, blk]).astype(BF16)

    v_t = (kv[:, LANES:2 * LANES] * r_x).T
    ones_row = (lax.broadcasted_iota(jnp.int32, (HEAD_DIM, tm), 0) == 0).astype(F32)
    for g in range(N_KV_HEADS):
        v_g = v_t[g * HEAD_DIM:(g + 1) * HEAD_DIM, :]
        vt[2 * g, :, BLOCK:BLOCK + tm] = jnp.concatenate([v_g, ones_row], axis=0).astype(BF16)
        vt[2 * g + 1, :, BLOCK:BLOCK + tm] = jnp.concatenate([ones_row, v_g], axis=0).astype(BF16)

    pairs = [(b, g) for b in range(nblk) for g in range(N_KV_HEADS)]

    key_i = lax.broadcasted_iota(jnp.int32, (2 * BLOCK, COLS_PER_KV * BLOCK), 0)
    qry_i = lax.broadcasted_iota(jnp.int32, (2 * BLOCK, COLS_PER_KV * BLOCK), 1) & (BLOCK - 1)
    band = (key_i > qry_i) & (key_i <= qry_i + BLOCK)
    first_band = band & (key_i >= jnp.where(j == 0, BLOCK, 0))
    cap = jnp.where(band, jnp.inf, NEG_INF)
    first_cap = jnp.where(first_band, jnp.inf, NEG_INF)

    def scores(i):
        b, g = pairs[i]
        for half in range(2):
            keys = kk[2 * g + half, b * BLOCK:(b + 2) * BLOCK, :]
            sbuf[2 * i + half] = jnp.dot(keys, qq[g, b], preferred_element_type=F32)

    sink_term = {}

    def softmax(i):
        b, g = pairs[i]
        for half in range(2):
            s = jnp.minimum(sbuf[2 * i + half], first_cap if b == 0 else cap)
            sink = sink_ref[2 * g + half:2 * g + half + 1, :]
            m = jnp.maximum(jnp.max(s, axis=0, keepdims=True), sink)
            sink_term[2 * i + half] = jnp.exp2(sink - m)
            pbuf[2 * i + half] = jnp.exp2(s - m).astype(BF16)

    def weighted_values(i):
        b, g = pairs[i]
        keys = slice(b * BLOCK, (b + 2) * BLOCK)
        o_lo = jnp.dot(vt[2 * g, :, keys], pbuf[2 * i], preferred_element_type=F32)
        o_hi = jnp.dot(vt[2 * g + 1, :, keys], pbuf[2 * i + 1], preferred_element_type=F32)
        inv_lo = 1.0 / (o_lo[HEAD_DIM:HEAD_DIM + 1, :] + sink_term[2 * i])
        inv_hi = 1.0 / (o_hi[0:1, :] + sink_term[2 * i + 1])
        o_t = jnp.concatenate([o_lo[0:HEAD_DIM, :] * inv_lo, o_hi[HEAD_DIM:2 * HEAD_DIM, :] * inv_hi], axis=0)
        for r in range(COLS_PER_KV):
            c = COLS_PER_KV * g + r
            abuf[b * BLOCK:(b + 1) * BLOCK, c * LANES:(c + 1) * LANES] = o_t[:, r * BLOCK:(r + 1) * BLOCK].T

    n_iter, n_conv = len(pairs) + 2, len(conv_steps)
    for i in range(n_iter):
        while len(conv_steps) * n_iter > (n_iter - 1 - i) * n_conv:
            conv_steps.pop(0)()
        if i < len(pairs):
            scores(i)
        if 0 <= i - 1 < len(pairs):
            softmax(i - 1)
        if i - 2 >= 0:
            weighted_values(i - 2)

    kk[:, 0:BLOCK, :] = kk[:, tm:tm + BLOCK, :]
    vt[:, :, 0:BLOCK] = vt[:, :, tm:tm + BLOCK]

    mix_a = _rms(abuf[...], aog_ref[this_layer, :]).astype(BF16)
    out_ref[0] = out_c[0] + jnp.dot(mix_a, wo_ref[c3:c3 + ATTN_WIDTH, :], preferred_element_type=F32)


def _ffn_kernel(x_ref, g_ref, wg_ref, wu_ref, wd_ref, out_ref, *, layer):
    n_sub = x_ref.shape[0] // FF_ROWS
    chunks = [(lo, min(lo + FF_CHUNK, D_FF)) for lo in range(0, D_FF, FF_CHUNK)]
    items = [(r, c) for r in range(n_sub) for c in range(len(chunks))]

    def rows(r):
        return slice(r * FF_ROWS, (r + 1) * FF_ROWS)

    def normed(r):
        x = x_ref[rows(r), :]
        inv_rms = lax.rsqrt(jnp.mean(x * x, axis=-1, keepdims=True) + EPS)
        return (x * g_ref[layer:layer + 1, :]).astype(BF16), inv_rms

    def gate_up(h, c):
        lo, hi = chunks[c]
        xg, inv_rms = h
        return (jnp.dot(xg, wg_ref[:, lo:hi], preferred_element_type=F32) * inv_rms,
                jnp.dot(xg, wu_ref[:, lo:hi], preferred_element_type=F32) * inv_rms)

    h = {0: normed(0)}
    nxt = gate_up(h[0], 0)
    acc = None
    for k, (r, c) in enumerate(items):
        gate, up = nxt
        if c == 0:
            acc = x_ref[rows(r), :]
            if r + 1 < n_sub:
                h[r + 1] = normed(r + 1)
        if k + 1 < len(items):
            r1, c1 = items[k + 1]
            nxt = gate_up(h[r1], c1)
        act = (gate * jax.nn.sigmoid(gate) * up).astype(BF16)
        lo, hi = chunks[c]
        acc = acc + jnp.dot(act, wd_ref[lo:hi, :], preferred_element_type=F32)
        if c == len(chunks) - 1:
            out_ref[rows(r), :] = acc


def _resident(shape, index_map):
    return pl.BlockSpec(shape, index_map, pipeline_mode=pl.Buffered(1))


def _mixer(x, g1, w_in, conv_w, kg2, sink_rows, cog, aog, w_o, to_cast, layer, tm):
    b, seq, d = x.shape
    depth = g1.shape[0]
    nj = seq // tm
    steps = b * nj
    lsel3 = lambda bi, ji: (layer, 0, 0)
    top2 = lambda bi, ji: (0, 0)
    vec = lambda n: _resident((depth, n), top2)
    n_units = (tm // BLOCK) * N_KV_HEADS * 2
    cast_in, cast_out, cast_shapes = [], [], []
    for w, wl in to_cast:
        rows, cols = w.shape[1] // steps, w.shape[2]
        assert rows * steps == w.shape[1] and rows % (2 * SUBLANES) == 0
        cast_in.append(pl.BlockSpec((None, rows, cols), lambda bi, ji, wl=wl: (wl, bi * nj + ji, 0)))
        cast_out.append(pl.BlockSpec((rows, cols), lambda bi, ji: (bi * nj + ji, 0)))
        cast_shapes.append(jax.ShapeDtypeStruct(w.shape[1:], BF16))
    outs = pl.pallas_call(
        functools.partial(_mixer_kernel, tm=tm, n_cast=len(to_cast), layer=layer),
        name=f"mixer_l{layer}",
        grid=(b, nj),
        in_specs=[
            pl.BlockSpec((1, tm, d), lambda bi, ji: (bi, ji, 0)),
            vec(d),
            _resident((d, IN_COLS), top2),
            _resident((None, SUBLANES, CONV_CHANNELS), lsel3),
            vec(LANES),
            _resident((None, SUBLANES, COLS_PER_KV * BLOCK), lsel3),
            vec(CONV_CHANNELS),
            vec(ATTN_WIDTH),
            _resident((CONV_CHANNELS + ATTN_WIDTH, d), top2),
        ] + cast_in,
        out_specs=[pl.BlockSpec((1, tm, d), lambda bi, ji: (bi, ji, 0))] + cast_out,
        out_shape=[jax.ShapeDtypeStruct(x.shape, x.dtype)] + cast_shapes,
        scratch_shapes=[
            pltpu.VMEM((SUBLANES, CONV_CHANNELS), F32),
            pltpu.VMEM((N_KV_HEADS, tm // BLOCK, LANES, COLS_PER_KV * BLOCK), BF16),
            pltpu.VMEM((4, tm + BLOCK, LANES), BF16),
            pltpu.VMEM((4, LANES, tm + BLOCK), BF16),
            pltpu.VMEM((n_units, 2 * BLOCK, COLS_PER_KV * BLOCK), F32),
            pltpu.VMEM((n_units, 2 * BLOCK, COLS_PER_KV * BLOCK), BF16),
            pltpu.VMEM((tm, ATTN_WIDTH), F32),
            pltpu.VMEM((tm, CONV_CHANNELS), F32),
            pltpu.VMEM((ATTN_WIDTH, d), BF16),
        ],
        compiler_params=pltpu.CompilerParams(
            dimension_semantics=("arbitrary", "arbitrary"),
            vmem_limit_bytes=VMEM_LIMIT_BYTES,
        ),
    )(x, g1, w_in, conv_w, kg2, sink_rows, cog, aog, w_o, *[w for w, _ in to_cast])
    return outs[0], outs[1:]


def _ffn(x2d, g, w_gate, w_up, w_down, layer, tm):
    n, d = x2d.shape
    top2 = lambda i: (0, 0)
    return pl.pallas_call(
        functools.partial(_ffn_kernel, layer=layer),
        name=f"ffn_l{layer}",
        grid=(n // tm,),
        in_specs=[
            pl.BlockSpec((tm, d), lambda i: (i, 0)),
            _resident(g.shape, top2),
            _resident((d, D_FF), top2),
            _resident((d, D_FF), top2),
            _resident((D_FF, d), top2),
        ],
        out_specs=pl.BlockSpec((tm, d), lambda i: (i, 0)),
        out_shape=jax.ShapeDtypeStruct(x2d.shape, x2d.dtype),
        compiler_params=pltpu.CompilerParams(
            dimension_semantics=("arbitrary",),
            vmem_limit_bytes=VMEM_LIMIT_BYTES,
        ),
    )(x2d, g, w_gate, w_up, w_down)


def kernel(x, norm1_g, w_in, conv_w, q_norm_g, k_norm_g, sinks, conv_out_g, attn_out_g, w_o, norm2_g,
           w_gate, w_up, w_down):
    b, seq, d = x.shape
    depth = w_in.shape[0]
    assert seq % TM_MIXER == 0 and (b * seq) % TM_FFN == 0 and TM_MIXER % BLOCK == 0

    sink_rows = jnp.repeat((sinks * LOG2E).reshape(depth, N_KV_HEADS, COLS_PER_KV, 2).transpose(0, 1, 3, 2)
                           .reshape(depth, 2 * N_KV_HEADS, COLS_PER_KV), BLOCK, axis=-1)
    sink_rows = jnp.pad(sink_rows, ((0, 0), (0, SUBLANES - 2 * N_KV_HEADS), (0, 0)))
    kg2 = jnp.tile(k_norm_g * q_norm_g * (HEAD_DIM ** -0.5 * LOG2E), (1, 2))
    conv_w8 = jnp.pad(conv_w, ((0, 0), (0, SUBLANES - CONV_WIDTH), (0, 0)))
    w_in_b, w_o_b = w_in[0].astype(BF16), w_o[0].astype(BF16)

    for l in range(depth):
        to_cast = [(w_gate, l), (w_up, l), (w_down, l)]
        if l + 1 < depth:
            to_cast += [(w_in, l + 1), (w_o, l + 1)]
        x, cast = _mixer(x, norm1_g, w_in_b, conv_w8, kg2, sink_rows, conv_out_g, attn_out_g, w_o_b, to_cast, l,
                         TM_MIXER)
        w_gate_b, w_up_b, w_down_b = cast[:3]
        if l + 1 < depth:
            w_in_b, w_o_b = cast[3:]
        x = _ffn(x.reshape(b * seq, d), norm2_g, w_gate_b, w_up_b, w_down_b, l, TM_FFN).reshape(b, seq, d)
    return x
```

```python
import functools

import jax
import jax.numpy as jnp
from jax import lax
from jax.experimental import pallas as pl
from jax.experimental.pallas import tpu as pltpu

D_MODEL = 1024
CONV_CHANNELS = 512
CONV_WIDTH = 3
N_Q_HEADS = 8
N_KV_HEADS = 2
HEAD_DIM = 64
ATTN_WIDTH = N_Q_HEADS * HEAD_DIM
BLOCK = 128
IN_COLS = 3 * CONV_CHANNELS + (N_Q_HEADS + 2 * N_KV_HEADS) * HEAD_DIM
D_FF = 2816
EPS = 1e-6
NEG_INF = -1e30

LANES = 128
SUBLANES = 8
Q_COLS = ATTN_WIDTH // LANES
COLS_PER_KV = Q_COLS // N_KV_HEADS
LOG2E = 1.4426950408889634
KV_OFF = 3 * CONV_CHANNELS
VMEM_LIMIT_BYTES = 56 * 1024 * 1024

TM_MIXER = 1024
CONV_CHUNK = 256
N_MIXER_IN = 9
TM_FFN = 1024
FF_ROWS = 512
FF_CHUNK = 1536

F32 = jnp.float32
BF16 = jnp.bfloat16


def _rms(x, g):
    ms = jnp.mean(x * x, axis=-1, keepdims=True)
    return x * lax.rsqrt(ms + EPS) * g


def _mixer_kernel(*refs, tm, n_cast, layer, f32_weights):
    (x_ref, g1_ref, win_ref, convw_ref, kg_ref, sink_ref, cog_ref, aog_ref, wo_ref) = refs[:N_MIXER_IN]
    cast_src = refs[N_MIXER_IN:N_MIXER_IN + n_cast]
    out_ref = refs[N_MIXER_IN + n_cast]
    cast_dst = refs[N_MIXER_IN + n_cast + 1:N_MIXER_IN + 2 * n_cast + 1]
    ubuf, qq, kk, vt, sbuf, pbuf, abuf, cbuf, wqt, *own_bf16 = refs[N_MIXER_IN + 2 * n_cast + 1:]
    first_step = (pl.program_id(0) == 0) & (pl.program_id(1) == 0)
    if f32_weights:
        @pl.when(first_step)
        def _():
            for src, dst in zip((win_ref, wo_ref), own_bf16):
                dst[...] = src[...].astype(BF16)
        win_ref, wo_ref = own_bf16
    j = pl.program_id(1)
    this_layer = slice(layer, layer + 1)

    for src, dst in zip(cast_src, cast_dst):
        dst[...] = src[...].astype(BF16)
    c3 = CONV_CHANNELS
    nblk = tm // BLOCK

    @pl.when(first_step)
    def _():
        wqt[...] = win_ref[:, KV_OFF:KV_OFF + ATTN_WIDTH].astype(F32).T.astype(BF16)

    @pl.when(j == 0)
    def _():
        ubuf[0:SUBLANES, :] = jnp.zeros((SUBLANES, c3), F32)
        kk[:, 0:BLOCK, :] = jnp.zeros((4, BLOCK, LANES), BF16)
        vt[:, :, 0:BLOCK] = jnp.zeros((4, LANES, BLOCK), BF16)

    h = _rms(x_ref[0], g1_ref[this_layer, :]).astype(BF16)

    def in_proj(lo, hi):
        return jnp.dot(h, win_ref[:, lo:hi], preferred_element_type=F32)

    def conv_finish(lo, hi, b_gate, c_gate, hc):
        u = c_gate * hc
        w0, w1, w2 = (convw_ref[t:t + 1, lo:hi] for t in range(CONV_WIDTH))
        y = w0 * pltpu.roll(u, 2, 0) + w1 * pltpu.roll(u, 1, 0) + w2 * u
        head = jnp.concatenate([ubuf[:, lo:hi], u[0:SUBLANES, :]], axis=0)
        y_head = (w0 * head[SUBLANES - 2:2 * SUBLANES - 2, :] + w1 * head[SUBLANES - 1:2 * SUBLANES - 1, :]
                  + w2 * u[0:SUBLANES, :])
        ubuf[:, lo:hi] = u[tm - SUBLANES:tm, :]
        cbuf[:, lo:hi] = b_gate * jnp.concatenate([y_head, y[SUBLANES:, :]], axis=0)

    conv_steps = []
    for lo in range(0, c3, CONV_CHUNK):
        parts = []
        for base in (0, c3, 2 * c3):
            conv_steps.append(lambda lo=lo, base=base, parts=parts: parts.append(
                in_proj(base + lo, base + lo + CONV_CHUNK)))
        conv_steps.append(lambda lo=lo, parts=parts: conv_finish(lo, lo + CONV_CHUNK, *parts))
    out_c = []
    conv_steps.append(lambda: out_c.append(
        x_ref[0] + jnp.dot(_rms(cbuf[...], cog_ref[this_layer, :]).astype(BF16), wo_ref[0:c3, :], preferred_element_type=F32)))

    kv = in_proj(KV_OFF + ATTN_WIDTH, IN_COLS)
    q_t = lax.dot_general(wqt[...], h, (((1,), (1,)), ((), ())), preferred_element_type=F32)
    lane = lax.broadcasted_iota(jnp.int32, (1, LANES), 1)
    lo_half = lane < HEAD_DIM

    k_sq = kv[:, 0:LANES] * kv[:, 0:LANES]
    ms_lo = jnp.sum(jnp.where(lo_half, k_sq, 0.0), axis=-1, keepdims=True)
    ms_hi = jnp.sum(jnp.where(lo_half, 0.0, k_sq), axis=-1, keepdims=True)
    kn = kv[:, 0:LANES] * lax.rsqrt(jnp.where(lo_half, ms_lo, ms_hi) * (1.0 / HEAD_DIM) + EPS) * kg_ref[this_layer, :]
    k_a0 = jnp.where(lo_half, kn, 0.0)
    k_b1 = jnp.where(lo_half, 0.0, kn)
    kk[0, BLOCK:BLOCK + tm, :] = k_a0.astype(BF16)
    kk[1, BLOCK:BLOCK + tm, :] = pltpu.roll(k_a0, HEAD_DIM, 1).astype(BF16)
    kk[2, BLOCK:BLOCK + tm, :] = pltpu.roll(k_b1, HEAD_DIM, 1).astype(BF16)
    kk[3, BLOCK:BLOCK + tm, :] = k_b1.astype(BF16)

    for hd in range(N_Q_HEADS):
        c, hf = divmod(hd, 2)
        g, r = divmod(c, COLS_PER_KV)
        t = q_t[hd * HEAD_DIM:(hd + 1) * HEAD_DIM, :]
        inv = lax.rsqrt(jnp.mean(t * t, axis=0, keepdims=True) + EPS)
        for b in range(nblk):
            blk = slice(b * BLOCK, (b + 1) * BLOCK)
            qq[g, b, hf * HEAD_DIM:(hf + 1) * HEAD_DIM, r * BLOCK:(r + 1) * BLOCK] = (
                t[:, blk] * inv[:, blk]).astype(BF16)

    v_t = kv[:, LANES:2 * LANES].T
    ones_row = (lax.broadcasted_iota(jnp.int32, (HEAD_DIM, tm), 0) == 0).astype(F32)
    for g in range(N_KV_HEADS):
        v_g = v_t[g * HEAD_DIM:(g + 1) * HEAD_DIM, :]
        vt[2 * g, :, BLOCK:BLOCK + tm] = jnp.concatenate([v_g, ones_row], axis=0).astype(BF16)
        vt[2 * g + 1, :, BLOCK:BLOCK + tm] = jnp.concatenate([ones_row, v_g], axis=0).astype(BF16)

    pairs = [(b, g) for b in range(nblk) for g in range(N_KV_HEADS)]

    key_i = lax.broadcasted_iota(jnp.int32, (2 * BLOCK, COLS_PER_KV * BLOCK), 0)
    qry_i = lax.broadcasted_iota(jnp.int32, (2 * BLOCK, COLS_PER_KV * BLOCK), 1) & (BLOCK - 1)
    band = (key_i > qry_i) & (key_i <= qry_i + BLOCK)
    first_band = band & (key_i >= jnp.where(j == 0, BLOCK, 0))
    cap = jnp.where(band, jnp.inf, NEG_INF)
    first_cap = jnp.where(first_band, jnp.inf, NEG_INF)

    def scores(i):
        b, g = pairs[i]
        for half in range(2):
            keys = kk[2 * g + half, b * BLOCK:(b + 2) * BLOCK, :]
            sbuf[2 * i + half] = jnp.dot(keys, qq[g, b], preferred_element_type=F32)

    sink_term = {}

    def softmax(i):
        b, g = pairs[i]
        for half in range(2):
            s = jnp.minimum(sbuf[2 * i + half], first_cap if b == 0 else cap)
            sink = sink_ref[2 * g + half:2 * g + half + 1, :]
            m = jnp.maximum(jnp.max(s, axis=0, keepdims=True), sink)
            sink_term[2 * i + half] = jnp.exp2(sink - m)
            pbuf[2 * i + half] = jnp.exp2(s - m).astype(BF16)

    def weighted_values(i):
        b, g = pairs[i]
        keys = slice(b * BLOCK, (b + 2) * BLOCK)
        o_lo = jnp.dot(vt[2 * g, :, keys], pbuf[2 * i], preferred_element_type=F32)
        o_hi = jnp.dot(vt[2 * g + 1, :, keys], pbuf[2 * i + 1], preferred_element_type=F32)
        inv_lo = 1.0 / (o_lo[HEAD_DIM:HEAD_DIM + 1, :] + sink_term[2 * i])
        inv_hi = 1.0 / (o_hi[0:1, :] + sink_term[2 * i + 1])
        o_t = jnp.concatenate([o_lo[0:HEAD_DIM, :] * inv_lo, o_hi[HEAD_DIM:2 * HEAD_DIM, :] * inv_hi], axis=0)
        for r in range(COLS_PER_KV):
            c = COLS_PER_KV * g + r
            abuf[b * BLOCK:(b + 1) * BLOCK, c * LANES:(c + 1) * LANES] = o_t[:, r * BLOCK:(r + 1) * BLOCK].T

    n_iter, n_conv = len(pairs) + 2, len(conv_steps)
    for i in range(n_iter):
        while len(conv_steps) * n_iter > (n_iter - 1 - i) * n_conv:
            conv_steps.pop(0)()
        if i < len(pairs):
            scores(i)
        if 0 <= i - 1 < len(pairs):
            softmax(i - 1)
        if i - 2 >= 0:
            weighted_values(i - 2)

    kk[:, 0:BLOCK, :] = kk[:, tm:tm + BLOCK, :]
    vt[:, :, 0:BLOCK] = vt[:, :, tm:tm + BLOCK]

    mix_a = _rms(abuf[...], aog_ref[this_layer, :]).astype(BF16)
    out_ref[0] = out_c[0] + jnp.dot(mix_a, wo_ref[c3:c3 + ATTN_WIDTH, :], preferred_element_type=F32)


def _ffn_kernel(x_ref, g_ref, wg_ref, wu_ref, wd_ref, out_ref, *, layer):
    n_sub = x_ref.shape[0] // FF_ROWS
    chunks = [(lo, min(lo + FF_CHUNK, D_FF)) for lo in range(0, D_FF, FF_CHUNK)]
    items = [(r, c) for r in range(n_sub) for c in range(len(chunks))]

    def rows(r):
        return slice(r * FF_ROWS, (r + 1) * FF_ROWS)

    def normed(r):
        x = x_ref[rows(r), :]
        inv_rms = lax.rsqrt(jnp.mean(x * x, axis=-1, keepdims=True) + EPS)
        return (x * g_ref[layer:layer + 1, :]).astype(BF16), inv_rms

    def gate_up(h, c):
        lo, hi = chunks[c]
        xg, inv_rms = h
        return (jnp.dot(xg, wg_ref[:, lo:hi], preferred_element_type=F32) * inv_rms,
                jnp.dot(xg, wu_ref[:, lo:hi], preferred_element_type=F32) * inv_rms)

    h = {0: normed(0)}
    nxt = gate_up(h[0], 0)
    acc = None
    for k, (r, c) in enumerate(items):
        gate, up = nxt
        if c == 0:
            acc = x_ref[rows(r), :]
            if r + 1 < n_sub:
                h[r + 1] = normed(r + 1)
        if k + 1 < len(items):
            r1, c1 = items[k + 1]
            nxt = gate_up(h[r1], c1)
        act = (gate * jax.nn.sigmoid(gate) * up).astype(BF16)
        lo, hi = chunks[c]
        acc = acc + jnp.dot(act, wd_ref[lo:hi, :], preferred_element_type=F32)
        if c == len(chunks) - 1:
            out_ref[rows(r), :] = acc


def _resident(shape, index_map):
    return pl.BlockSpec(shape, index_map, pipeline_mode=pl.Buffered(1))


def _mixer(x, g1, w_in, conv_w, kg2, sink_rows, cog, aog, w_o, to_cast, layer, tm):
    b, seq, d = x.shape
    depth = g1.shape[0]
    nj = seq // tm
    steps = b * nj
    lsel3 = lambda bi, ji: (layer, 0, 0)
    top2 = lambda bi, ji: (0, 0)
    vec = lambda n: _resident((depth, n), top2)
    n_units = (tm // BLOCK) * N_KV_HEADS * 2
    cast_in, cast_out, cast_shapes = [], [], []
    for w, wl in to_cast:
        rows, cols = w.shape[1] // steps, w.shape[2]
        assert rows * steps == w.shape[1] and rows % (2 * SUBLANES) == 0
        cast_in.append(pl.BlockSpec((None, rows, cols), lambda bi, ji, wl=wl: (wl, bi * nj + ji, 0)))
        cast_out.append(pl.BlockSpec((rows, cols), lambda bi, ji: (bi * nj + ji, 0)))
        cast_shapes.append(jax.ShapeDtypeStruct(w.shape[1:], BF16))
    f32_weights = w_in.dtype != BF16
    if f32_weights:
        w_in_spec, w_o_spec = _resident((None, d, IN_COLS), lsel3), _resident((None, CONV_CHANNELS + ATTN_WIDTH, d), lsel3)
        own_bf16 = [pltpu.VMEM((d, IN_COLS), BF16), pltpu.VMEM((CONV_CHANNELS + ATTN_WIDTH, d), BF16)]
    else:
        w_in_spec, w_o_spec = _resident((d, IN_COLS), top2), _resident((CONV_CHANNELS + ATTN_WIDTH, d), top2)
        own_bf16 = []
    outs = pl.pallas_call(
        functools.partial(_mixer_kernel, tm=tm, n_cast=len(to_cast), layer=layer, f32_weights=f32_weights),
        name=f"mixer_l{layer}",
        grid=(b, nj),
        in_specs=[
            pl.BlockSpec((1, tm, d), lambda bi, ji: (bi, ji, 0)),
            vec(d),
            w_in_spec,
            _resident((None, SUBLANES, CONV_CHANNELS), lsel3),
            vec(LANES),
            _resident((None, SUBLANES, COLS_PER_KV * BLOCK), lsel3),
            vec(CONV_CHANNELS),
            vec(ATTN_WIDTH),
            w_o_spec,
        ] + cast_in,
        out_specs=[pl.BlockSpec((1, tm, d), lambda bi, ji: (bi, ji, 0))] + cast_out,
        out_shape=[jax.ShapeDtypeStruct(x.shape, x.dtype)] + cast_shapes,
        scratch_shapes=[
            pltpu.VMEM((SUBLANES, CONV_CHANNELS), F32),
            pltpu.VMEM((N_KV_HEADS, tm // BLOCK, LANES, COLS_PER_KV * BLOCK), BF16),
            pltpu.VMEM((4, tm + BLOCK, LANES), BF16),
            pltpu.VMEM((4, LANES, tm + BLOCK), BF16),
            pltpu.VMEM((n_units, 2 * BLOCK, COLS_PER_KV * BLOCK), F32),
            pltpu.VMEM((n_units, 2 * BLOCK, COLS_PER_KV * BLOCK), BF16),
            pltpu.VMEM((tm, ATTN_WIDTH), F32),
            pltpu.VMEM((tm, CONV_CHANNELS), F32),
            pltpu.VMEM((ATTN_WIDTH, d), BF16),
        ] + own_bf16,
        compiler_params=pltpu.CompilerParams(
            dimension_semantics=("arbitrary", "arbitrary"),
            vmem_limit_bytes=VMEM_LIMIT_BYTES,
        ),
    )(x, g1, w_in, conv_w, kg2, sink_rows, cog, aog, w_o, *[w for w, _ in to_cast])
    return outs[0], outs[1:]


def _ffn(x2d, g, w_gate, w_up, w_down, layer, tm):
    n, d = x2d.shape
    top2 = lambda i: (0, 0)
    return pl.pallas_call(
        functools.partial(_ffn_kernel, layer=layer),
        name=f"ffn_l{layer}",
        grid=(n // tm,),
        in_specs=[
            pl.BlockSpec((tm, d), lambda i: (i, 0)),
            _resident(g.shape, top2),
            _resident((d, D_FF), top2),
            _resident((d, D_FF), top2),
            _resident((D_FF, d), top2),
        ],
        out_specs=pl.BlockSpec((tm, d), lambda i: (i, 0)),
        out_shape=jax.ShapeDtypeStruct(x2d.shape, x2d.dtype),
        compiler_params=pltpu.CompilerParams(
            dimension_semantics=("arbitrary",),
            vmem_limit_bytes=VMEM_LIMIT_BYTES,
        ),
    )(x2d, g, w_gate, w_up, w_down)


def kernel(x, norm1_g, w_in, conv_w, q_norm_g, k_norm_g, sinks, conv_out_g, attn_out_g, w_o, norm2_g,
           w_gate, w_up, w_down):
    b, seq, d = x.shape
    depth = w_in.shape[0]
    assert seq % TM_MIXER == 0 and (b * seq) % TM_FFN == 0 and TM_MIXER % BLOCK == 0

    sink_rows = jnp.repeat((sinks * LOG2E).reshape(depth, N_KV_HEADS, COLS_PER_KV, 2).transpose(0, 1, 3, 2)
                           .reshape(depth, 2 * N_KV_HEADS, COLS_PER_KV), BLOCK, axis=-1)
    sink_rows = jnp.pad(sink_rows, ((0, 0), (0, SUBLANES - 2 * N_KV_HEADS), (0, 0)))
    kg2 = jnp.tile(k_norm_g * q_norm_g * (HEAD_DIM ** -0.5 * LOG2E), (1, 2))
    conv_w8 = jnp.pad(conv_w, ((0, 0), (0, SUBLANES - CONV_WIDTH), (0, 0)))
    w_in_b, w_o_b = w_in, w_o

    for l in range(depth):
        to_cast = [(w_gate, l), (w_up, l), (w_down, l)]
        if l + 1 < depth:
            to_cast += [(w_in, l + 1), (w_o, l + 1)]
        x, cast = _mixer(x, norm1_g, w_in_b, conv_w8, kg2, sink_rows, conv_out_g, attn_out_g, w_o_b, to_cast, l,
                         TM_MIXER)
        w_gate_b, w_up_b, w_down_b = cast[:3]
        if l + 1 < depth:
            w_in_b, w_o_b = cast[3:]
        x = _ffn(x.reshape(b * seq, d), norm2_g, w_gate_b, w_up_b, w_down_b, l, TM_FFN).reshape(b, seq, d)
    return x
```

```python
import functools

import jax
import jax.numpy as jnp
from jax import lax
from jax.experimental import pallas as pl
from jax.experimental.pallas import tpu as pltpu

D_MODEL = 1024
CONV_CHANNELS = 512
CONV_WIDTH = 3
N_Q_HEADS = 8
N_KV_HEADS = 2
HEAD_DIM = 64
ATTN_WIDTH = N_Q_HEADS * HEAD_DIM
BLOCK = 128
IN_COLS = 3 * CONV_CHANNELS + (N_Q_HEADS + 2 * N_KV_HEADS) * HEAD_DIM
D_FF = 2816
EPS = 1e-6
NEG_INF = -1e30

LANES = 128
SUBLANES = 8
Q_COLS = ATTN_WIDTH // LANES
COLS_PER_KV = Q_COLS // N_KV_HEADS
LOG2E = 1.4426950408889634
KV_OFF = 3 * CONV_CHANNELS
VMEM_LIMIT_BYTES = 56 * 1024 * 1024

TM_MIXER = 1024
CONV_CHUNK = 256
N_MIXER_IN = 9
TM_FFN = 1024
FF_ROWS = 512
FF_CHUNK = 1536

F32 = jnp.float32
BF16 = jnp.bfloat16


def _rms(x, g):
    ms = jnp.mean(x * x, axis=-1, keepdims=True)
    return x * lax.rsqrt(ms + EPS) * g


def _mixer_kernel(*refs, tm, n_cast, layer, f32_weights):
    (x_ref, g1_ref, win_ref, convw_ref, kg_ref, sink_ref, cog_ref, aog_ref, wo_ref) = refs[:N_MIXER_IN]
    cast_src = refs[N_MIXER_IN:N_MIXER_IN + n_cast]
    out_ref = refs[N_MIXER_IN + n_cast]
    cast_dst = refs[N_MIXER_IN + n_cast + 1:N_MIXER_IN + 2 * n_cast + 1]
    ubuf, qq, kk, vt, sbuf, pbuf, abuf, cbuf, wqt, *own_bf16 = refs[N_MIXER_IN + 2 * n_cast + 1:]
    first_step = (pl.program_id(0) == 0) & (pl.program_id(1) == 0)
    if f32_weights:
        @pl.when(first_step)
        def _():
            for src, dst in zip((win_ref, wo_ref), own_bf16):
                dst[...] = src[...].astype(BF16)
        win_ref, wo_ref = own_bf16
    j = pl.program_id(1)
    this_layer = slice(layer, layer + 1)

    for src, dst in zip(cast_src, cast_dst):
        dst[...] = src[...].astype(BF16)
    c3 = CONV_CHANNELS
    nblk = tm // BLOCK

    @pl.when(first_step)
    def _():
        wqt[...] = win_ref[:, KV_OFF:KV_OFF + ATTN_WIDTH].astype(F32).T.astype(BF16)

    @pl.when(j == 0)
    def _():
        ubuf[0:SUBLANES, :] = jnp.zeros((SUBLANES, c3), F32)
        kk[:, 0:BLOCK, :] = jnp.zeros((4, BLOCK, LANES), BF16)
        vt[:, :, 0:BLOCK] = jnp.zeros((4, LANES, BLOCK), BF16)

    h = _rms(x_ref[0], g1_ref[this_layer, :]).astype(BF16)

    def in_proj(lo, hi):
        return jnp.dot(h, win_ref[:, lo:hi], preferred_element_type=F32)

    def conv_finish(lo, hi, b_gate, c_gate, hc):
        u = c_gate * hc
        w0, w1, w2 = (convw_ref[layer, t:t + 1, lo:hi] for t in range(CONV_WIDTH))
        y = w0 * pltpu.roll(u, 2, 0) + w1 * pltpu.roll(u, 1, 0) + w2 * u
        head = jnp.concatenate([ubuf[:, lo:hi], u[0:SUBLANES, :]], axis=0)
        y_head = (w0 * head[SUBLANES - 2:2 * SUBLANES - 2, :] + w1 * head[SUBLANES - 1:2 * SUBLANES - 1, :]
                  + w2 * u[0:SUBLANES, :])
        ubuf[:, lo:hi] = u[tm - SUBLANES:tm, :]
        cbuf[:, lo:hi] = b_gate * jnp.concatenate([y_head, y[SUBLANES:, :]], axis=0)

    conv_steps = []
    for lo in range(0, c3, CONV_CHUNK):
        parts = []
        for base in (0, c3, 2 * c3):
            conv_steps.append(lambda lo=lo, base=base, parts=parts: parts.append(
                in_proj(base + lo, base + lo + CONV_CHUNK)))
        conv_steps.append(lambda lo=lo, parts=parts: conv_finish(lo, lo + CONV_CHUNK, *parts))
    out_c = []
    conv_steps.append(lambda: out_c.append(
        x_ref[0] + jnp.dot(_rms(cbuf[...], cog_ref[this_layer, :]).astype(BF16), wo_ref[0:c3, :], preferred_element_type=F32)))

    kv = in_proj(KV_OFF + ATTN_WIDTH, IN_COLS)
    q_t = lax.dot_general(wqt[...], h, (((1,), (1,)), ((), ())), preferred_element_type=F32)
    lane = lax.broadcasted_iota(jnp.int32, (1, LANES), 1)
    lo_half = lane < HEAD_DIM

    k_sq = kv[:, 0:LANES] * kv[:, 0:LANES]
    ms_lo = jnp.sum(jnp.where(lo_half, k_sq, 0.0), axis=-1, keepdims=True)
    ms_hi = jnp.sum(jnp.where(lo_half, 0.0, k_sq), axis=-1, keepdims=True)
    kn = kv[:, 0:LANES] * lax.rsqrt(jnp.where(lo_half, ms_lo, ms_hi) * (1.0 / HEAD_DIM) + EPS) * kg_ref[this_layer, :]
    k_a0 = jnp.where(lo_half, kn, 0.0)
    k_b1 = jnp.where(lo_half, 0.0, kn)
    kk[0, BLOCK:BLOCK + tm, :] = k_a0.astype(BF16)
    kk[1, BLOCK:BLOCK + tm, :] = pltpu.roll(k_a0, HEAD_DIM, 1).astype(BF16)
    kk[2, BLOCK:BLOCK + tm, :] = pltpu.roll(k_b1, HEAD_DIM, 1).astype(BF16)
    kk[3, BLOCK:BLOCK + tm, :] = k_b1.astype(BF16)

    for hd in range(N_Q_HEADS):
        c, hf = divmod(hd, 2)
        g, r = divmod(c, COLS_PER_KV)
        t = q_t[hd * HEAD_DIM:(hd + 1) * HEAD_DIM, :]
        inv = lax.rsqrt(jnp.mean(t * t, axis=0, keepdims=True) + EPS)
        for b in range(nblk):
            blk = slice(b * BLOCK, (b + 1) * BLOCK)
            qq[g, b, hf * HEAD_DIM:(hf + 1) * HEAD_DIM, r * BLOCK:(r + 1) * BLOCK] = (
                t[:, blk] * inv[:, blk]).astype(BF16)

    v_t = kv[:, LANES:2 * LANES].T
    ones_row = (lax.broadcasted_iota(jnp.int32, (HEAD_DIM, tm), 0) == 0).astype(F32)
    for g in range(N_KV_HEADS):
        v_g = v_t[g * HEAD_DIM:(g + 1) * HEAD_DIM, :]
        vt[2 * g, :, BLOCK:BLOCK + tm] = jnp.concatenate([v_g, ones_row], axis=0).astype(BF16)
        vt[2 * g + 1, :, BLOCK:BLOCK + tm] = jnp.concatenate([ones_row, v_g], axis=0).astype(BF16)

    pairs = [(b, g) for b in range(nblk) for g in range(N_KV_HEADS)]

    key_i = lax.broadcasted_iota(jnp.int32, (2 * BLOCK, COLS_PER_KV * BLOCK), 0)
    qry_i = lax.broadcasted_iota(jnp.int32, (2 * BLOCK, COLS_PER_KV * BLOCK), 1) & (BLOCK - 1)
    band = (key_i > qry_i) & (key_i <= qry_i + BLOCK)
    first_band = band & (key_i >= jnp.where(j == 0, BLOCK, 0))
    cap = jnp.where(band, jnp.inf, NEG_INF)
    first_col = lax.broadcasted_iota(jnp.int32, (1, COLS_PER_KV * BLOCK), 1) < BLOCK
    first_cap = jnp.where(first_band, jnp.inf, NEG_INF)

    def scores(i):
        b, g = pairs[i]
        for half in range(2):
            keys = kk[2 * g + half, b * BLOCK:(b + 2) * BLOCK, :]
            sbuf[2 * i + half] = jnp.dot(keys, qq[g, b], preferred_element_type=F32)

    sink_term = {}

    def softmax(i):
        b, g = pairs[i]
        for half in range(2):
            s = jnp.minimum(sbuf[2 * i + half], first_cap if b == 0 else cap)
            sink = jnp.where(first_col, sink_ref[layer, 4 * g + half] * LOG2E,
                             sink_ref[layer, 4 * g + 2 + half] * LOG2E)
            m = jnp.maximum(jnp.max(s, axis=0, keepdims=True), sink)
            sink_term[2 * i + half] = jnp.exp2(sink - m)
            pbuf[2 * i + half] = jnp.exp2(s - m).astype(BF16)

    def weighted_values(i):
        b, g = pairs[i]
        keys = slice(b * BLOCK, (b + 2) * BLOCK)
        o_lo = jnp.dot(vt[2 * g, :, keys], pbuf[2 * i], preferred_element_type=F32)
        o_hi = jnp.dot(vt[2 * g + 1, :, keys], pbuf[2 * i + 1], preferred_element_type=F32)
        inv_lo = 1.0 / (o_lo[HEAD_DIM:HEAD_DIM + 1, :] + sink_term[2 * i])
        inv_hi = 1.0 / (o_hi[0:1, :] + sink_term[2 * i + 1])
        o_t = jnp.concatenate([o_lo[0:HEAD_DIM, :] * inv_lo, o_hi[HEAD_DIM:2 * HEAD_DIM, :] * inv_hi], axis=0)
        for r in range(COLS_PER_KV):
            c = COLS_PER_KV * g + r
            abuf[b * BLOCK:(b + 1) * BLOCK, c * LANES:(c + 1) * LANES] = o_t[:, r * BLOCK:(r + 1) * BLOCK].T

    n_iter, n_conv = len(pairs) + 2, len(conv_steps)
    for i in range(n_iter):
        while len(conv_steps) * n_iter > (n_iter - 1 - i) * n_conv:
            conv_steps.pop(0)()
        if i < len(pairs):
            scores(i)
        if 0 <= i - 1 < len(pairs):
            softmax(i - 1)
        if i - 2 >= 0:
            weighted_values(i - 2)

    kk[:, 0:BLOCK, :] = kk[:, tm:tm + BLOCK, :]
    vt[:, :, 0:BLOCK] = vt[:, :, tm:tm + BLOCK]

    mix_a = _rms(abuf[...], aog_ref[this_layer, :]).astype(BF16)
    out_ref[0] = out_c[0] + jnp.dot(mix_a, wo_ref[c3:c3 + ATTN_WIDTH, :], preferred_element_type=F32)


def _ffn_kernel(x_ref, g_ref, wg_ref, wu_ref, wd_ref, out_ref, *, layer):
    n_sub = x_ref.shape[0] // FF_ROWS
    chunks = [(lo, min(lo + FF_CHUNK, D_FF)) for lo in range(0, D_FF, FF_CHUNK)]
    items = [(r, c) for r in range(n_sub) for c in range(len(chunks))]

    def rows(r):
        return slice(r * FF_ROWS, (r + 1) * FF_ROWS)

    def normed(r):
        x = x_ref[rows(r), :]
        inv_rms = lax.rsqrt(jnp.mean(x * x, axis=-1, keepdims=True) + EPS)
        return (x * g_ref[layer:layer + 1, :]).astype(BF16), inv_rms

    def gate_up(h, c):
        lo, hi = chunks[c]
        xg, inv_rms = h
        return (jnp.dot(xg, wg_ref[:, lo:hi], preferred_element_type=F32) * inv_rms,
                jnp.dot(xg, wu_ref[:, lo:hi], preferred_element_type=F32) * inv_rms)

    h = {0: normed(0)}
    nxt = gate_up(h[0], 0)
    acc = None
    for k, (r, c) in enumerate(items):
        gate, up = nxt
        if c == 0:
            acc = x_ref[rows(r), :]
            if r + 1 < n_sub:
                h[r + 1] = normed(r + 1)
        if k + 1 < len(items):
            r1, c1 = items[k + 1]
            nxt = gate_up(h[r1], c1)
        act = (gate * jax.nn.sigmoid(gate) * up).astype(BF16)
        lo, hi = chunks[c]
        acc = acc + jnp.dot(act, wd_ref[lo:hi, :], preferred_element_type=F32)
        if c == len(chunks) - 1:
            out_ref[rows(r), :] = acc


def _resident(shape, index_map):
    return pl.BlockSpec(shape, index_map, pipeline_mode=pl.Buffered(1))


def _mixer(x, g1, w_in, conv_w, kg2, sinks, cog, aog, w_o, to_cast, layer, tm):
    b, seq, d = x.shape
    depth = g1.shape[0]
    nj = seq // tm
    steps = b * nj
    lsel3 = lambda bi, ji: (layer, 0, 0)
    top2 = lambda bi, ji: (0, 0)
    vec = lambda n: _resident((depth, n), top2)
    n_units = (tm // BLOCK) * N_KV_HEADS * 2
    cast_in, cast_out, cast_shapes = [], [], []
    for w, wl in to_cast:
        rows, cols = w.shape[1] // steps, w.shape[2]
        assert rows * steps == w.shape[1] and rows % (2 * SUBLANES) == 0
        cast_in.append(pl.BlockSpec((None, rows, cols), lambda bi, ji, wl=wl: (wl, bi * nj + ji, 0)))
        cast_out.append(pl.BlockSpec((rows, cols), lambda bi, ji: (bi * nj + ji, 0)))
        cast_shapes.append(jax.ShapeDtypeStruct(w.shape[1:], BF16))
    f32_weights = w_in.dtype != BF16
    if f32_weights:
        w_in_spec, w_o_spec = _resident((None, d, IN_COLS), lsel3), _resident((None, CONV_CHANNELS + ATTN_WIDTH, d), lsel3)
        own_bf16 = [pltpu.VMEM((d, IN_COLS), BF16), pltpu.VMEM((CONV_CHANNELS + ATTN_WIDTH, d), BF16)]
    else:
        w_in_spec, w_o_spec = _resident((d, IN_COLS), top2), _resident((CONV_CHANNELS + ATTN_WIDTH, d), top2)
        own_bf16 = []
    outs = pl.pallas_call(
        functools.partial(_mixer_kernel, tm=tm, n_cast=len(to_cast), layer=layer, f32_weights=f32_weights),
        name=f"mixer_l{layer}",
        grid=(b, nj),
        in_specs=[
            pl.BlockSpec((1, tm, d), lambda bi, ji: (bi, ji, 0)),
            vec(d),
            w_in_spec,
            _resident(conv_w.shape, lambda bi, ji: (0, 0, 0)),
            vec(LANES),
            pl.BlockSpec(memory_space=pltpu.SMEM),
            vec(CONV_CHANNELS),
            vec(ATTN_WIDTH),
            w_o_spec,
        ] + cast_in,
        out_specs=[pl.BlockSpec((1, tm, d), lambda bi, ji: (bi, ji, 0))] + cast_out,
        out_shape=[jax.ShapeDtypeStruct(x.shape, x.dtype)] + cast_shapes,
        scratch_shapes=[
            pltpu.VMEM((SUBLANES, CONV_CHANNELS), F32),
            pltpu.VMEM((N_KV_HEADS, tm // BLOCK, LANES, COLS_PER_KV * BLOCK), BF16),
            pltpu.VMEM((4, tm + BLOCK, LANES), BF16),
            pltpu.VMEM((4, LANES, tm + BLOCK), BF16),
            pltpu.VMEM((n_units, 2 * BLOCK, COLS_PER_KV * BLOCK), F32),
            pltpu.VMEM((n_units, 2 * BLOCK, COLS_PER_KV * BLOCK), BF16),
            pltpu.VMEM((tm, ATTN_WIDTH), F32),
            pltpu.VMEM((tm, CONV_CHANNELS), F32),
            pltpu.VMEM((ATTN_WIDTH, d), BF16),
        ] + own_bf16,
        compiler_params=pltpu.CompilerParams(
            dimension_semantics=("arbitrary", "arbitrary"),
            vmem_limit_bytes=VMEM_LIMIT_BYTES,
        ),
    )(x, g1, w_in, conv_w, kg2, sinks, cog, aog, w_o, *[w for w, _ in to_cast])
    return outs[0], outs[1:]


def _ffn(x2d, g, w_gate, w_up, w_down, layer, tm):
    n, d = x2d.shape
    top2 = lambda i: (0, 0)
    return pl.pallas_call(
        functools.partial(_ffn_kernel, layer=layer),
        name=f"ffn_l{layer}",
        grid=(n // tm,),
        in_specs=[
            pl.BlockSpec((tm, d), lambda i: (i, 0)),
            _resident(g.shape, top2),
            _resident((d, D_FF), top2),
            _resident((d, D_FF), top2),
            _resident((D_FF, d), top2),
        ],
        out_specs=pl.BlockSpec((tm, d), lambda i: (i, 0)),
        out_shape=jax.ShapeDtypeStruct(x2d.shape, x2d.dtype),
        compiler_params=pltpu.CompilerParams(
            dimension_semantics=("arbitrary",),
            vmem_limit_bytes=VMEM_LIMIT_BYTES,
        ),
    )(x2d, g, w_gate, w_up, w_down)


def kernel(x, norm1_g, w_in, conv_w, q_norm_g, k_norm_g, sinks, conv_out_g, attn_out_g, w_o, norm2_g,
           w_gate, w_up, w_down):
    b, seq, d = x.shape
    depth = w_in.shape[0]
    assert seq % TM_MIXER == 0 and (b * seq) % TM_FFN == 0 and TM_MIXER % BLOCK == 0

    kg2 = jnp.tile(k_norm_g * q_norm_g * (HEAD_DIM ** -0.5 * LOG2E), (1, 2))
    w_in_b, w_o_b = w_in, w_o

    for l in range(depth):
        to_cast = [(w_gate, l), (w_up, l), (w_down, l)]
        if l + 1 < depth:
            to_cast += [(w_in, l + 1), (w_o, l + 1)]
        x, cast = _mixer(x, norm1_g, w_in_b, conv_w, kg2, sinks, conv_out_g, attn_out_g, w_o_b, to_cast, l,
                         TM_MIXER)
        w_gate_b, w_up_b, w_down_b = cast[:3]
        if l + 1 < depth:
            w_in_b, w_o_b = cast[3:]
        x = _ffn(x.reshape(b * seq, d), norm2_g, w_gate_b, w_up_b, w_down_b, l, TM_FFN).reshape(b, seq, d)
    return x
```

```python
import functools

import jax
import jax.numpy as jnp
from jax import lax
from jax.experimental import pallas as pl
from jax.experimental.pallas import tpu as pltpu

D_MODEL = 1024
CONV_CHANNELS = 512
CONV_WIDTH = 3
N_Q_HEADS = 8
N_KV_HEADS = 2
HEAD_DIM = 64
ATTN_WIDTH = N_Q_HEADS * HEAD_DIM
BLOCK = 128
IN_COLS = 3 * CONV_CHANNELS + (N_Q_HEADS + 2 * N_KV_HEADS) * HEAD_DIM
D_FF = 2816
EPS = 1e-6
NEG_INF = -1e30

LANES = 128
SUBLANES = 8
Q_COLS = ATTN_WIDTH // LANES
COLS_PER_KV = Q_COLS // N_KV_HEADS
LOG2E = 1.4426950408889634
KV_OFF = 3 * CONV_CHANNELS
VMEM_LIMIT_BYTES = 56 * 1024 * 1024

TM_MIXER = 1024
CONV_CHUNK = 256
N_MIXER_IN = 9
TM_FFN = 1024
FF_ROWS = 256
FF_CHUNK = 1536

F32 = jnp.float32
BF16 = jnp.bfloat16


def _rms(x, g):
    ms = jnp.mean(x * x, axis=-1, keepdims=True)
    return x * lax.rsqrt(ms + EPS) * g


def _mixer_kernel(*refs, tm, n_cast, layer, f32_weights):
    (x_ref, g1_ref, win_ref, convw_ref, kg_ref, sink_ref, cog_ref, aog_ref, wo_ref) = refs[:N_MIXER_IN]
    cast_src = refs[N_MIXER_IN:N_MIXER_IN + n_cast]
    out_ref = refs[N_MIXER_IN + n_cast]
    cast_dst = refs[N_MIXER_IN + n_cast + 1:N_MIXER_IN + 2 * n_cast + 1]
    ubuf, qq, kk, vt, sbuf, pbuf, abuf, cbuf, wqt, *own_bf16 = refs[N_MIXER_IN + 2 * n_cast + 1:]
    first_step = (pl.program_id(0) == 0) & (pl.program_id(1) == 0)
    if f32_weights:
        @pl.when(first_step)
        def _():
            for src, dst in zip((win_ref, wo_ref), own_bf16):
                dst[...] = src[...].astype(BF16)
        win_ref, wo_ref = own_bf16
    j = pl.program_id(1)
    this_layer = slice(layer, layer + 1)

    for src, dst in zip(cast_src, cast_dst):
        dst[...] = src[...].astype(BF16)
    c3 = CONV_CHANNELS
    nblk = tm // BLOCK

    @pl.when(first_step)
    def _():
        wqt[...] = win_ref[:, KV_OFF:KV_OFF + ATTN_WIDTH].astype(F32).T.astype(BF16)

    @pl.when(j == 0)
    def _():
        ubuf[0:SUBLANES, :] = jnp.zeros((SUBLANES, c3), F32)
        kk[:, 0:BLOCK, :] = jnp.zeros((4, BLOCK, LANES), BF16)
        vt[:, :, 0:BLOCK] = jnp.zeros((4, LANES, BLOCK), BF16)

    h = _rms(x_ref[0], g1_ref[this_layer, :]).astype(BF16)

    def in_proj(lo, hi):
        return jnp.dot(h, win_ref[:, lo:hi], preferred_element_type=F32)

    def conv_finish(lo, hi, b_gate, c_gate, hc):
        u = c_gate * hc
        w0, w1, w2 = (convw_ref[layer, t:t + 1, lo:hi] for t in range(CONV_WIDTH))
        y = w0 * pltpu.roll(u, 2, 0) + w1 * pltpu.roll(u, 1, 0) + w2 * u
        head = jnp.concatenate([ubuf[:, lo:hi], u[0:SUBLANES, :]], axis=0)
        y_head = (w0 * head[SUBLANES - 2:2 * SUBLANES - 2, :] + w1 * head[SUBLANES - 1:2 * SUBLANES - 1, :]
                  + w2 * u[0:SUBLANES, :])
        ubuf[:, lo:hi] = u[tm - SUBLANES:tm, :]
        cbuf[:, lo:hi] = b_gate * jnp.concatenate([y_head, y[SUBLANES:, :]], axis=0)

    conv_steps = []
    for lo in range(0, c3, CONV_CHUNK):
        parts = []
        for base in (0, c3, 2 * c3):
            conv_steps.append(lambda lo=lo, base=base, parts=parts: parts.append(
                in_proj(base + lo, base + lo + CONV_CHUNK)))
        conv_steps.append(lambda lo=lo, parts=parts: conv_finish(lo, lo + CONV_CHUNK, *parts))
    out_c = []
    conv_steps.append(lambda: out_c.append(
        x_ref[0] + jnp.dot(_rms(cbuf[...], cog_ref[this_layer, :]).astype(BF16), wo_ref[0:c3, :], preferred_element_type=F32)))

    kv = in_proj(KV_OFF + ATTN_WIDTH, IN_COLS)
    q_t = lax.dot_general(wqt[...], h, (((1,), (1,)), ((), ())), preferred_element_type=F32)
    lane = lax.broadcasted_iota(jnp.int32, (1, LANES), 1)
    lo_half = lane < HEAD_DIM

    k_sq = kv[:, 0:LANES] * kv[:, 0:LANES]
    ms_lo = jnp.sum(jnp.where(lo_half, k_sq, 0.0), axis=-1, keepdims=True)
    ms_hi = jnp.sum(jnp.where(lo_half, 0.0, k_sq), axis=-1, keepdims=True)
    kn = kv[:, 0:LANES] * lax.rsqrt(jnp.where(lo_half, ms_lo, ms_hi) * (1.0 / HEAD_DIM) + EPS) * kg_ref[this_layer, :]
    k_a0 = jnp.where(lo_half, kn, 0.0)
    k_b1 = jnp.where(lo_half, 0.0, kn)
    kk[0, BLOCK:BLOCK + tm, :] = k_a0.astype(BF16)
    kk[1, BLOCK:BLOCK + tm, :] = pltpu.roll(k_a0, HEAD_DIM, 1).astype(BF16)
    kk[2, BLOCK:BLOCK + tm, :] = pltpu.roll(k_b1, HEAD_DIM, 1).astype(BF16)
    kk[3, BLOCK:BLOCK + tm, :] = k_b1.astype(BF16)

    for hd in range(N_Q_HEADS):
        c, hf = divmod(hd, 2)
        g, r = divmod(c, COLS_PER_KV)
        t = q_t[hd * HEAD_DIM:(hd + 1) * HEAD_DIM, :]
        inv = lax.rsqrt(jnp.mean(t * t, axis=0, keepdims=True) + EPS)
        for b in range(nblk):
            blk = slice(b * BLOCK, (b + 1) * BLOCK)
            qq[g, b, hf * HEAD_DIM:(hf + 1) * HEAD_DIM, r * BLOCK:(r + 1) * BLOCK] = (
                t[:, blk] * inv[:, blk]).astype(BF16)

    v_t = kv[:, LANES:2 * LANES].T
    ones_row = (lax.broadcasted_iota(jnp.int32, (HEAD_DIM, tm), 0) == 0).astype(F32)
    for g in range(N_KV_HEADS):
        v_g = v_t[g * HEAD_DIM:(g + 1) * HEAD_DIM, :]
        vt[2 * g, :, BLOCK:BLOCK + tm] = jnp.concatenate([v_g, ones_row], axis=0).astype(BF16)
        vt[2 * g + 1, :, BLOCK:BLOCK + tm] = jnp.concatenate([ones_row, v_g], axis=0).astype(BF16)

    pairs = [(b, g) for b in range(nblk) for g in range(N_KV_HEADS)]

    key_i = lax.broadcasted_iota(jnp.int32, (2 * BLOCK, COLS_PER_KV * BLOCK), 0)
    qry_i = lax.broadcasted_iota(jnp.int32, (2 * BLOCK, COLS_PER_KV * BLOCK), 1) & (BLOCK - 1)
    band = (key_i > qry_i) & (key_i <= qry_i + BLOCK)
    first_band = band & (key_i >= jnp.where(j == 0, BLOCK, 0))
    cap = jnp.where(band, jnp.inf, NEG_INF)
    first_col = lax.broadcasted_iota(jnp.int32, (1, COLS_PER_KV * BLOCK), 1) < BLOCK
    first_cap = jnp.where(first_band, jnp.inf, NEG_INF)

    def scores(i):
        b, g = pairs[i]
        for half in range(2):
            keys = kk[2 * g + half, b * BLOCK:(b + 2) * BLOCK, :]
            sbuf[2 * i + half] = jnp.dot(keys, qq[g, b], preferred_element_type=F32)

    sink_term = {}

    def softmax(i):
        b, g = pairs[i]
        for half in range(2):
            s = jnp.minimum(sbuf[2 * i + half], first_cap if b == 0 else cap)
            sink = jnp.where(first_col, sink_ref[layer, 4 * g + half] * LOG2E,
                             sink_ref[layer, 4 * g + 2 + half] * LOG2E)
            m = jnp.maximum(jnp.max(s, axis=0, keepdims=True), sink)
            sink_term[2 * i + half] = jnp.exp2(sink - m)
            pbuf[2 * i + half] = jnp.exp2(s - m).astype(BF16)

    def weighted_values(i):
        b, g = pairs[i]
        keys = slice(b * BLOCK, (b + 2) * BLOCK)
        o_lo = jnp.dot(vt[2 * g, :, keys], pbuf[2 * i], preferred_element_type=F32)
        o_hi = jnp.dot(vt[2 * g + 1, :, keys], pbuf[2 * i + 1], preferred_element_type=F32)
        inv_lo = 1.0 / (o_lo[HEAD_DIM:HEAD_DIM + 1, :] + sink_term[2 * i])
        inv_hi = 1.0 / (o_hi[0:1, :] + sink_term[2 * i + 1])
        o_t = jnp.concatenate([o_lo[0:HEAD_DIM, :] * inv_lo, o_hi[HEAD_DIM:2 * HEAD_DIM, :] * inv_hi], axis=0)
        for r in range(COLS_PER_KV):
            c = COLS_PER_KV * g + r
            abuf[b * BLOCK:(b + 1) * BLOCK, c * LANES:(c + 1) * LANES] = o_t[:, r * BLOCK:(r + 1) * BLOCK].T

    n_iter, n_conv = len(pairs) + 2, len(conv_steps)
    for i in range(n_iter):
        while len(conv_steps) * n_iter > (n_iter - 1 - i) * n_conv:
            conv_steps.pop(0)()
        if i < len(pairs):
            scores(i)
        if 0 <= i - 1 < len(pairs):
            softmax(i - 1)
        if i - 2 >= 0:
            weighted_values(i - 2)

    kk[:, 0:BLOCK, :] = kk[:, tm:tm + BLOCK, :]
    vt[:, :, 0:BLOCK] = vt[:, :, tm:tm + BLOCK]

    mix_a = _rms(abuf[...], aog_ref[this_layer, :]).astype(BF16)
    out_ref[0] = out_c[0] + jnp.dot(mix_a, wo_ref[c3:c3 + ATTN_WIDTH, :], preferred_element_type=F32)


def _ffn_kernel(x_ref, g_ref, wg_ref, wu_ref, wd_ref, out_ref, *, layer):
    n_sub = x_ref.shape[0] // FF_ROWS
    chunks = [(lo, min(lo + FF_CHUNK, D_FF)) for lo in range(0, D_FF, FF_CHUNK)]
    items = [(r, c) for r in range(n_sub) for c in range(len(chunks))]

    def rows(r):
        return slice(r * FF_ROWS, (r + 1) * FF_ROWS)

    def normed(r):
        x = x_ref[rows(r), :]
        inv_rms = lax.rsqrt(jnp.mean(x * x, axis=-1, keepdims=True) + EPS)
        return (x * g_ref[layer:layer + 1, :]).astype(BF16), inv_rms

    def gate_up(h, c):
        lo, hi = chunks[c]
        xg, inv_rms = h
        return (jnp.dot(xg, wg_ref[:, lo:hi], preferred_element_type=F32) * inv_rms,
                jnp.dot(xg, wu_ref[:, lo:hi], preferred_element_type=F32) * inv_rms)

    h = {0: normed(0)}
    nxt = gate_up(h[0], 0)
    acc = None
    for k, (r, c) in enumerate(items):
        gate, up = nxt
        if c == 0:
            acc = x_ref[rows(r), :]
            if r + 1 < n_sub:
                h[r + 1] = normed(r + 1)
        if k + 1 < len(items):
            r1, c1 = items[k + 1]
            nxt = gate_up(h[r1], c1)
        act = (gate * jax.nn.sigmoid(gate) * up).astype(BF16)
        lo, hi = chunks[c]
        acc = acc + jnp.dot(act, wd_ref[lo:hi, :], preferred_element_type=F32)
        if c == len(chunks) - 1:
            out_ref[rows(r), :] = acc


def _resident(shape, index_map):
    return pl.BlockSpec(shape, index_map, pipeline_mode=pl.Buffered(1))


def _mixer(x, g1, w_in, conv_w, kg2, sinks, cog, aog, w_o, to_cast, layer, tm):
    b, seq, d = x.shape
    depth = g1.shape[0]
    nj = seq // tm
    steps = b * nj
    lsel3 = lambda bi, ji: (layer, 0, 0)
    top2 = lambda bi, ji: (0, 0)
    vec = lambda n: _resident((depth, n), top2)
    n_units = (tm // BLOCK) * N_KV_HEADS * 2
    cast_in, cast_out, cast_shapes = [], [], []
    for w, wl in to_cast:
        rows, cols = w.shape[1] // steps, w.shape[2]
        assert rows * steps == w.shape[1] and rows % (2 * SUBLANES) == 0
        cast_in.append(pl.BlockSpec((None, rows, cols), lambda bi, ji, wl=wl: (wl, bi * nj + ji, 0)))
        cast_out.append(pl.BlockSpec((rows, cols), lambda bi, ji: (bi * nj + ji, 0)))
        cast_shapes.append(jax.ShapeDtypeStruct(w.shape[1:], BF16))
    f32_weights = w_in.dtype != BF16
    if f32_weights:
        w_in_spec, w_o_spec = _resident((None, d, IN_COLS), lsel3), _resident((None, CONV_CHANNELS + ATTN_WIDTH, d), lsel3)
        own_bf16 = [pltpu.VMEM((d, IN_COLS), BF16), pltpu.VMEM((CONV_CHANNELS + ATTN_WIDTH, d), BF16)]
    else:
        w_in_spec, w_o_spec = _resident((d, IN_COLS), top2), _resident((CONV_CHANNELS + ATTN_WIDTH, d), top2)
        own_bf16 = []
    outs = pl.pallas_call(
        functools.partial(_mixer_kernel, tm=tm, n_cast=len(to_cast), layer=layer, f32_weights=f32_weights),
        name=f"mixer_l{layer}",
        grid=(b, nj),
        in_specs=[
            pl.BlockSpec((1, tm, d), lambda bi, ji: (bi, ji, 0)),
            vec(d),
            w_in_spec,
            _resident(conv_w.shape, lambda bi, ji: (0, 0, 0)),
            vec(LANES),
            pl.BlockSpec(memory_space=pltpu.SMEM),
            vec(CONV_CHANNELS),
            vec(ATTN_WIDTH),
            w_o_spec,
        ] + cast_in,
        out_specs=[pl.BlockSpec((1, tm, d), lambda bi, ji: (bi, ji, 0))] + cast_out,
        out_shape=[jax.ShapeDtypeStruct(x.shape, x.dtype)] + cast_shapes,
        scratch_shapes=[
            pltpu.VMEM((SUBLANES, CONV_CHANNELS), F32),
            pltpu.VMEM((N_KV_HEADS, tm // BLOCK, LANES, COLS_PER_KV * BLOCK), BF16),
            pltpu.VMEM((4, tm + BLOCK, LANES), BF16),
            pltpu.VMEM((4, LANES, tm + BLOCK), BF16),
            pltpu.VMEM((n_units, 2 * BLOCK, COLS_PER_KV * BLOCK), F32),
            pltpu.VMEM((n_units, 2 * BLOCK, COLS_PER_KV * BLOCK), BF16),
            pltpu.VMEM((tm, ATTN_WIDTH), F32),
            pltpu.VMEM((tm, CONV_CHANNELS), F32),
            pltpu.VMEM((ATTN_WIDTH, d), BF16),
        ] + own_bf16,
        compiler_params=pltpu.CompilerParams(
            dimension_semantics=("arbitrary", "arbitrary"),
            vmem_limit_bytes=VMEM_LIMIT_BYTES,
        ),
    )(x, g1, w_in, conv_w, kg2, sinks, cog, aog, w_o, *[w for w, _ in to_cast])
    return outs[0], outs[1:]


def _ffn(x2d, g, w_gate, w_up, w_down, layer, tm):
    n, d = x2d.shape
    top2 = lambda i: (0, 0)
    return pl.pallas_call(
        functools.partial(_ffn_kernel, layer=layer),
        name=f"ffn_l{layer}",
        grid=(n // tm,),
        in_specs=[
            pl.BlockSpec((tm, d), lambda i: (i, 0)),
            _resident(g.shape, top2),
            _resident((d, D_FF), top2),
            _resident((d, D_FF), top2),
            _resident((D_FF, d), top2),
        ],
        out_specs=pl.BlockSpec((tm, d), lambda i: (i, 0)),
        out_shape=jax.ShapeDtypeStruct(x2d.shape, x2d.dtype),
        compiler_params=pltpu.CompilerParams(
            dimension_semantics=("arbitrary",),
            vmem_limit_bytes=VMEM_LIMIT_BYTES,
        ),
    )(x2d, g, w_gate, w_up, w_down)


def kernel(x, norm1_g, w_in, conv_w, q_norm_g, k_norm_g, sinks, conv_out_g, attn_out_g, w_o, norm2_g,
           w_gate, w_up, w_down):
    b, seq, d = x.shape
    depth = w_in.shape[0]
    assert seq % TM_MIXER == 0 and (b * seq) % TM_FFN == 0 and TM_MIXER % BLOCK == 0

    kg2 = jnp.tile(k_norm_g * q_norm_g * (HEAD_DIM ** -0.5 * LOG2E), (1, 2))
    w_in_b, w_o_b = w_in, w_o

    for l in range(depth):
        to_cast = [(w_gate, l), (w_up, l), (w_down, l)]
        if l + 1 < depth:
            to_cast += [(w_in, l + 1), (w_o, l + 1)]
        x, cast = _mixer(x, norm1_g, w_in_b, conv_w, kg2, sinks, conv_out_g, attn_out_g, w_o_b, to_cast, l,
                         TM_MIXER)
        w_gate_b, w_up_b, w_down_b = cast[:3]
        if l + 1 < depth:
            w_in_b, w_o_b = cast[3:]
        x = _ffn(x.reshape(b * seq, d), norm2_g, w_gate_b, w_up_b, w_down_b, l, TM_FFN).reshape(b, seq, d)
    return x
```

```python
import functools

import jax
import jax.numpy as jnp
from jax import lax
from jax.experimental import pallas as pl
from jax.experimental.pallas import tpu as pltpu

D_MODEL = 1024
CONV_CHANNELS = 512
CONV_WIDTH = 3
N_Q_HEADS = 8
N_KV_HEADS = 2
HEAD_DIM = 64
ATTN_WIDTH = N_Q_HEADS * HEAD_DIM
BLOCK = 128
IN_COLS = 3 * CONV_CHANNELS + (N_Q_HEADS + 2 * N_KV_HEADS) * HEAD_DIM
D_FF = 2816
EPS = 1e-6
NEG_INF = -1e30

LANES = 128
SUBLANES = 8
Q_COLS = ATTN_WIDTH // LANES
COLS_PER_KV = Q_COLS // N_KV_HEADS
LOG2E = 1.4426950408889634
KV_OFF = 3 * CONV_CHANNELS
VMEM_LIMIT_BYTES = 60 * 1024 * 1024

TM_MIXER = 1024
CONV_CHUNK = 256
N_MIXER_IN = 9
TM_FFN = 1024
FF_ROWS = 256
FF_CHUNK = 1024

F32 = jnp.float32
BF16 = jnp.bfloat16


def _rms(x, g):
    ms = jnp.mean(x * x, axis=-1, keepdims=True)
    return x * lax.rsqrt(ms + EPS) * g


def _mixer_kernel(*refs, tm, n_cast, layer, f32_weights):
    (x_ref, g1_ref, win_ref, convw_ref, kg_ref, sink_ref, cog_ref, aog_ref, wo_ref) = refs[:N_MIXER_IN]
    cast_src = refs[N_MIXER_IN:N_MIXER_IN + n_cast]
    out_ref = refs[N_MIXER_IN + n_cast]
    cast_dst = refs[N_MIXER_IN + n_cast + 1:N_MIXER_IN + 2 * n_cast + 1]
    ubuf, qq, kk, vt, sbuf, pbuf, abuf, cbuf, wqt, *own_bf16 = refs[N_MIXER_IN + 2 * n_cast + 1:]
    first_step = (pl.program_id(0) == 0) & (pl.program_id(1) == 0)
    if f32_weights:
        @pl.when(first_step)
        def _():
            for src, dst in zip((win_ref, wo_ref), own_bf16):
                dst[...] = src[...].astype(BF16)
        win_ref, wo_ref = own_bf16
    j = pl.program_id(1)
    this_layer = slice(layer, layer + 1)

    for src, dst in zip(cast_src, cast_dst):
        dst[...] = src[...].astype(BF16)
    c3 = CONV_CHANNELS
    nblk = tm // BLOCK

    @pl.when(first_step)
    def _():
        wqt[...] = win_ref[:, KV_OFF:KV_OFF + ATTN_WIDTH].astype(F32).T.astype(BF16)

    @pl.when(j == 0)
    def _():
        ubuf[0:SUBLANES, :] = jnp.zeros((SUBLANES, c3), F32)
        kk[:, 0:BLOCK, :] = jnp.zeros((4, BLOCK, LANES), BF16)
        vt[:, :, 0:BLOCK] = jnp.zeros((4, LANES, BLOCK), BF16)

    h = _rms(x_ref[0], g1_ref[this_layer, :]).astype(BF16)

    def in_proj(lo, hi):
        return jnp.dot(h, win_ref[:, lo:hi], preferred_element_type=F32)

    def conv_finish(lo, hi, b_gate, c_gate, hc):
        u = c_gate * hc
        w0, w1, w2 = (convw_ref[layer, t:t + 1, lo:hi] for t in range(CONV_WIDTH))
        y = w0 * pltpu.roll(u, 2, 0) + w1 * pltpu.roll(u, 1, 0) + w2 * u
        head = jnp.concatenate([ubuf[:, lo:hi], u[0:SUBLANES, :]], axis=0)
        y_head = (w0 * head[SUBLANES - 2:2 * SUBLANES - 2, :] + w1 * head[SUBLANES - 1:2 * SUBLANES - 1, :]
                  + w2 * u[0:SUBLANES, :])
        ubuf[:, lo:hi] = u[tm - SUBLANES:tm, :]
        cbuf[:, lo:hi] = b_gate * jnp.concatenate([y_head, y[SUBLANES:, :]], axis=0)

    conv_steps = []
    for lo in range(0, c3, CONV_CHUNK):
        parts = []
        for base in (0, c3, 2 * c3):
            conv_steps.append(lambda lo=lo, base=base, parts=parts: parts.append(
                in_proj(base + lo, base + lo + CONV_CHUNK)))
        conv_steps.append(lambda lo=lo, parts=parts: conv_finish(lo, lo + CONV_CHUNK, *parts))
    proj_c = []
    conv_steps.append(lambda: proj_c.append(
        jnp.dot((cbuf[...] * cog_ref[this_layer, :]).astype(BF16), wo_ref[0:c3, :], preferred_element_type=F32)))

    kv = in_proj(KV_OFF + ATTN_WIDTH, IN_COLS)
    q_t = lax.dot_general(wqt[...], h, (((1,), (1,)), ((), ())), preferred_element_type=F32)
    lane = lax.broadcasted_iota(jnp.int32, (1, LANES), 1)
    lo_half = lane < HEAD_DIM

    k_sq = kv[:, 0:LANES] * kv[:, 0:LANES]
    ms_lo = jnp.sum(jnp.where(lo_half, k_sq, 0.0), axis=-1, keepdims=True)
    ms_hi = jnp.sum(jnp.where(lo_half, 0.0, k_sq), axis=-1, keepdims=True)
    kn = kv[:, 0:LANES] * lax.rsqrt(jnp.where(lo_half, ms_lo, ms_hi) * (1.0 / HEAD_DIM) + EPS) * kg_ref[this_layer, :]
    k_a0 = jnp.where(lo_half, kn, 0.0)
    k_b1 = jnp.where(lo_half, 0.0, kn)
    kk[0, BLOCK:BLOCK + tm, :] = k_a0.astype(BF16)
    kk[1, BLOCK:BLOCK + tm, :] = pltpu.roll(k_a0, HEAD_DIM, 1).astype(BF16)
    kk[2, BLOCK:BLOCK + tm, :] = pltpu.roll(k_b1, HEAD_DIM, 1).astype(BF16)
    kk[3, BLOCK:BLOCK + tm, :] = k_b1.astype(BF16)

    for hd in range(N_Q_HEADS):
        c, hf = divmod(hd, 2)
        g, r = divmod(c, COLS_PER_KV)
        t = q_t[hd * HEAD_DIM:(hd + 1) * HEAD_DIM, :]
        inv = lax.rsqrt(jnp.mean(t * t, axis=0, keepdims=True) + EPS)
        for b in range(nblk):
            blk = slice(b * BLOCK, (b + 1) * BLOCK)
            qq[g, b, hf * HEAD_DIM:(hf + 1) * HEAD_DIM, r * BLOCK:(r + 1) * BLOCK] = (
                t[:, blk] * inv[:, blk]).astype(BF16)

    v_t = kv[:, LANES:2 * LANES].T
    ones_row = (lax.broadcasted_iota(jnp.int32, (HEAD_DIM, tm), 0) == 0).astype(F32)
    for g in range(N_KV_HEADS):
        v_g = v_t[g * HEAD_DIM:(g + 1) * HEAD_DIM, :]
        vt[2 * g, :, BLOCK:BLOCK + tm] = jnp.concatenate([v_g, ones_row], axis=0).astype(BF16)
        vt[2 * g + 1, :, BLOCK:BLOCK + tm] = jnp.concatenate([ones_row, v_g], axis=0).astype(BF16)

    pairs = [(b, g) for b in range(nblk) for g in range(N_KV_HEADS)]

    key_i = lax.broadcasted_iota(jnp.int32, (2 * BLOCK, COLS_PER_KV * BLOCK), 0)
    qry_i = lax.broadcasted_iota(jnp.int32, (2 * BLOCK, COLS_PER_KV * BLOCK), 1) & (BLOCK - 1)
    band = (key_i > qry_i) & (key_i <= qry_i + BLOCK)
    first_band = band & (key_i >= jnp.where(j == 0, BLOCK, 0))
    cap = jnp.where(band, jnp.inf, NEG_INF)
    first_col = lax.broadcasted_iota(jnp.int32, (1, COLS_PER_KV * BLOCK), 1) < BLOCK
    first_cap = jnp.where(first_band, jnp.inf, NEG_INF)

    def scores(i):
        b, g = pairs[i]
        for half in range(2):
            keys = kk[2 * g + half, b * BLOCK:(b + 2) * BLOCK, :]
            sbuf[2 * i + half] = jnp.dot(keys, qq[g, b], preferred_element_type=F32)

    sink_term = {}

    def softmax(i):
        b, g = pairs[i]
        for half in range(2):
            s = jnp.minimum(sbuf[2 * i + half], first_cap if b == 0 else cap)
            sink = jnp.where(first_col, sink_ref[layer, 4 * g + half] * LOG2E,
                             sink_ref[layer, 4 * g + 2 + half] * LOG2E)
            m = jnp.maximum(jnp.max(s, axis=0, keepdims=True), sink)
            sink_term[2 * i + half] = jnp.exp2(sink - m)
            pbuf[2 * i + half] = jnp.exp2(s - m).astype(BF16)

    def weighted_values(i):
        b, g = pairs[i]
        keys = slice(b * BLOCK, (b + 2) * BLOCK)
        o_lo = jnp.dot(vt[2 * g, :, keys], pbuf[2 * i], preferred_element_type=F32)
        o_hi = jnp.dot(vt[2 * g + 1, :, keys], pbuf[2 * i + 1], preferred_element_type=F32)
        inv_lo = 1.0 / (o_lo[HEAD_DIM:HEAD_DIM + 1, :] + sink_term[2 * i])
        inv_hi = 1.0 / (o_hi[0:1, :] + sink_term[2 * i + 1])
        o_t = jnp.concatenate([o_lo[0:HEAD_DIM, :] * inv_lo, o_hi[HEAD_DIM:2 * HEAD_DIM, :] * inv_hi], axis=0)
        for r in range(COLS_PER_KV):
            c = COLS_PER_KV * g + r
            abuf[b * BLOCK:(b + 1) * BLOCK, c * LANES:(c + 1) * LANES] = o_t[:, r * BLOCK:(r + 1) * BLOCK].T

    n_iter, n_conv = len(pairs) + 2, len(conv_steps)
    for i in range(n_iter):
        while len(conv_steps) * n_iter > (n_iter - 1 - i) * n_conv:
            conv_steps.pop(0)()
        if i < len(pairs):
            scores(i)
        if 0 <= i - 1 < len(pairs):
            softmax(i - 1)
        if i - 2 >= 0:
            weighted_values(i - 2)

    kk[:, 0:BLOCK, :] = kk[:, tm:tm + BLOCK, :]
    vt[:, :, 0:BLOCK] = vt[:, :, tm:tm + BLOCK]

    mix_a = _rms(abuf[...], aog_ref[this_layer, :]).astype(BF16)
    conv_out = cbuf[...]
    inv_c = lax.rsqrt(jnp.mean(conv_out * conv_out, axis=-1, keepdims=True) + EPS)
    out_ref[0] = (x_ref[0] + proj_c[0] * inv_c
                  + jnp.dot(mix_a, wo_ref[c3:c3 + ATTN_WIDTH, :], preferred_element_type=F32))


def _ffn_kernel(x_ref, g_ref, wg_ref, wu_ref, wd_ref, out_ref, *, layer):
    n_sub = x_ref.shape[0] // FF_ROWS
    chunks = [(lo, min(lo + FF_CHUNK, D_FF)) for lo in range(0, D_FF, FF_CHUNK)]
    items = [(r, c) for r in range(n_sub) for c in range(len(chunks))]

    def rows(r):
        return slice(r * FF_ROWS, (r + 1) * FF_ROWS)

    def normed(r):
        x = x_ref[rows(r), :]
        inv_rms = lax.rsqrt(jnp.mean(x * x, axis=-1, keepdims=True) + EPS)
        return (x * g_ref[layer:layer + 1, :]).astype(BF16), inv_rms

    def gate_up(h, c):
        lo, hi = chunks[c]
        xg, inv_rms = h
        return (jnp.dot(xg, wg_ref[:, lo:hi], preferred_element_type=F32) * inv_rms,
                jnp.dot(xg, wu_ref[:, lo:hi], preferred_element_type=F32) * inv_rms)

    h = {0: normed(0)}
    nxt = gate_up(h[0], 0)
    acc = None
    for k, (r, c) in enumerate(items):
        gate, up = nxt
        if c == 0:
            acc = x_ref[rows(r), :]
            if r + 1 < n_sub:
                h[r + 1] = normed(r + 1)
        if k + 1 < len(items):
            r1, c1 = items[k + 1]
            nxt = gate_up(h[r1], c1)
        act = (gate * jax.nn.sigmoid(gate) * up).astype(BF16)
        lo, hi = chunks[c]
        acc = acc + jnp.dot(act, wd_ref[lo:hi, :], preferred_element_type=F32)
        if c == len(chunks) - 1:
            out_ref[rows(r), :] = acc


def _resident(shape, index_map):
    return pl.BlockSpec(shape, index_map, pipeline_mode=pl.Buffered(1))


def _mixer(x, g1, w_in, conv_w, kg2, sinks, cog, aog, w_o, to_cast, layer, tm):
    b, seq, d = x.shape
    depth = g1.shape[0]
    nj = seq // tm
    steps = b * nj
    lsel3 = lambda bi, ji: (layer, 0, 0)
    top2 = lambda bi, ji: (0, 0)
    vec = lambda n: _resident((depth, n), top2)
    n_units = (tm // BLOCK) * N_KV_HEADS * 2
    cast_in, cast_out, cast_shapes = [], [], []
    for w, wl in to_cast:
        rows, cols = w.shape[1] // steps, w.shape[2]
        assert rows * steps == w.shape[1] and rows % (2 * SUBLANES) == 0
        cast_in.append(pl.BlockSpec((None, rows, cols), lambda bi, ji, wl=wl: (wl, bi * nj + ji, 0)))
        cast_out.append(pl.BlockSpec((rows, cols), lambda bi, ji: (bi * nj + ji, 0)))
        cast_shapes.append(jax.ShapeDtypeStruct(w.shape[1:], BF16))
    f32_weights = w_in.dtype != BF16
    if f32_weights:
        w_in_spec, w_o_spec = _resident((None, d, IN_COLS), lsel3), _resident((None, CONV_CHANNELS + ATTN_WIDTH, d), lsel3)
        own_bf16 = [pltpu.VMEM((d, IN_COLS), BF16), pltpu.VMEM((CONV_CHANNELS + ATTN_WIDTH, d), BF16)]
    else:
        w_in_spec, w_o_spec = _resident((d, IN_COLS), top2), _resident((CONV_CHANNELS + ATTN_WIDTH, d), top2)
        own_bf16 = []
    outs = pl.pallas_call(
        functools.partial(_mixer_kernel, tm=tm, n_cast=len(to_cast), layer=layer, f32_weights=f32_weights),
        name=f"mixer_l{layer}",
        grid=(b, nj),
        in_specs=[
            pl.BlockSpec((1, tm, d), lambda bi, ji: (bi, ji, 0)),
            vec(d),
            w_in_spec,
            _resident(conv_w.shape, lambda bi, ji: (0, 0, 0)),
            vec(LANES),
            pl.BlockSpec(memory_space=pltpu.SMEM),
            vec(CONV_CHANNELS),
            vec(ATTN_WIDTH),
            w_o_spec,
        ] + cast_in,
        out_specs=[pl.BlockSpec((1, tm, d), lambda bi, ji: (bi, ji, 0))] + cast_out,
        out_shape=[jax.ShapeDtypeStruct(x.shape, x.dtype)] + cast_shapes,
        scratch_shapes=[
            pltpu.VMEM((SUBLANES, CONV_CHANNELS), F32),
            pltpu.VMEM((N_KV_HEADS, tm // BLOCK, LANES, COLS_PER_KV * BLOCK), BF16),
            pltpu.VMEM((4, tm + BLOCK, LANES), BF16),
            pltpu.VMEM((4, LANES, tm + BLOCK), BF16),
            pltpu.VMEM((n_units, 2 * BLOCK, COLS_PER_KV * BLOCK), F32),
            pltpu.VMEM((n_units, 2 * BLOCK, COLS_PER_KV * BLOCK), BF16),
            pltpu.VMEM((tm, ATTN_WIDTH), F32),
            pltpu.VMEM((tm, CONV_CHANNELS), F32),
            pltpu.VMEM((ATTN_WIDTH, d), BF16),
        ] + own_bf16,
        compiler_params=pltpu.CompilerParams(
            dimension_semantics=("arbitrary", "arbitrary"),
            vmem_limit_bytes=VMEM_LIMIT_BYTES,
        ),
    )(x, g1, w_in, conv_w, kg2, sinks, cog, aog, w_o, *[w for w, _ in to_cast])
    return outs[0], outs[1:]


def _ffn(x2d, g, w_gate, w_up, w_down, layer, tm):
    n, d = x2d.shape
    top2 = lambda i: (0, 0)
    return pl.pallas_call(
        functools.partial(_ffn_kernel, layer=layer),
        name=f"ffn_l{layer}",
        grid=(n // tm,),
        in_specs=[
            pl.BlockSpec((tm, d), lambda i: (i, 0)),
            _resident(g.shape, top2),
            _resident((d, D_FF), top2),
            _resident((d, D_FF), top2),
            _resident((D_FF, d), top2),
        ],
        out_specs=pl.BlockSpec((tm, d), lambda i: (i, 0)),
        out_shape=jax.ShapeDtypeStruct(x2d.shape, x2d.dtype),
        compiler_params=pltpu.CompilerParams(
            dimension_semantics=("arbitrary",),
            vmem_limit_bytes=VMEM_LIMIT_BYTES,
        ),
    )(x2d, g, w_gate, w_up, w_down)


def kernel(x, norm1_g, w_in, conv_w, q_norm_g, k_norm_g, sinks, conv_out_g, attn_out_g, w_o, norm2_g,
           w_gate, w_up, w_down):
    b, seq, d = x.shape
    depth = w_in.shape[0]
    assert seq % TM_MIXER == 0 and (b * seq) % TM_FFN == 0 and TM_MIXER % BLOCK == 0

    kg2 = jnp.tile(k_norm_g * q_norm_g * (HEAD_DIM ** -0.5 * LOG2E), (1, 2))
    w_in_b, w_o_b = w_in, w_o

    for l in range(depth):
        to_cast = [(w_gate, l), (w_up, l), (w_down, l)]
        if l + 1 < depth:
            to_cast += [(w_in, l + 1), (w_o, l + 1)]
        x, cast = _mixer(x, norm1_g, w_in_b, conv_w, kg2, sinks, conv_out_g, attn_out_g, w_o_b, to_cast, l,
                         TM_MIXER)
        w_gate_b, w_up_b, w_down_b = cast[:3]
        if l + 1 < depth:
            w_in_b, w_o_b = cast[3:]
        x = _ffn(x.reshape(b * seq, d), norm2_g, w_gate_b, w_up_b, w_down_b, l, TM_FFN).reshape(b, seq, d)
    return x
```

```python
import functools

import jax
import jax.numpy as jnp
from jax import lax
from jax.experimental import pallas as pl
from jax.experimental.pallas import tpu as pltpu

D_MODEL = 1024
CONV_CHANNELS = 512
CONV_WIDTH = 3
N_Q_HEADS = 8
N_KV_HEADS = 2
HEAD_DIM = 64
ATTN_WIDTH = N_Q_HEADS * HEAD_DIM
BLOCK = 128
IN_COLS = 3 * CONV_CHANNELS + (N_Q_HEADS + 2 * N_KV_HEADS) * HEAD_DIM
D_FF = 2816
EPS = 1e-6
NEG_INF = -1e30

LANES = 128
SUBLANES = 8
Q_COLS = ATTN_WIDTH // LANES
COLS_PER_KV = Q_COLS // N_KV_HEADS
LOG2E = 1.4426950408889634
KV_OFF = 3 * CONV_CHANNELS
VMEM_LIMIT_BYTES = 60 * 1024 * 1024

TM_MIXER = 1024
CONV_CHUNK = 256
FILL_START = 2
N_MIXER_IN = 9
TM_FFN = 1024
FF_ROWS = 256
FF_CHUNK = 1024

F32 = jnp.float32
BF16 = jnp.bfloat16


def _rms(x, g):
    ms = jnp.mean(x * x, axis=-1, keepdims=True)
    return x * lax.rsqrt(ms + EPS) * g


def _mixer_kernel(*refs, tm, n_cast, layer, f32_weights):
    (x_ref, g1_ref, win_ref, convw_ref, kg_ref, sink_ref, cog_ref, aog_ref, wo_ref) = refs[:N_MIXER_IN]
    cast_src = refs[N_MIXER_IN:N_MIXER_IN + n_cast]
    out_ref = refs[N_MIXER_IN + n_cast]
    cast_dst = refs[N_MIXER_IN + n_cast + 1:N_MIXER_IN + 2 * n_cast + 1]
    ubuf, qq, kk, vt, sbuf, pbuf, abuf, cbuf, wqt, *own_bf16 = refs[N_MIXER_IN + 2 * n_cast + 1:]
    first_step = (pl.program_id(0) == 0) & (pl.program_id(1) == 0)
    if f32_weights:
        @pl.when(first_step)
        def _():
            for src, dst in zip((win_ref, wo_ref), own_bf16):
                dst[...] = src[...].astype(BF16)
        win_ref, wo_ref = own_bf16
    j = pl.program_id(1)
    this_layer = slice(layer, layer + 1)

    for src, dst in zip(cast_src, cast_dst):
        dst[...] = src[...].astype(BF16)
    c3 = CONV_CHANNELS
    nblk = tm // BLOCK

    @pl.when(first_step)
    def _():
        wqt[...] = win_ref[:, KV_OFF:KV_OFF + ATTN_WIDTH].astype(F32).T.astype(BF16)

    @pl.when(j == 0)
    def _():
        ubuf[0:SUBLANES, :] = jnp.zeros((SUBLANES, c3), F32)
        kk[:, 0:BLOCK, :] = jnp.zeros((4, BLOCK, LANES), BF16)
        vt[:, :, 0:BLOCK] = jnp.zeros((4, LANES, BLOCK), BF16)

    h = _rms(x_ref[0], g1_ref[this_layer, :]).astype(BF16)

    def in_proj(lo, hi):
        return jnp.dot(h, win_ref[:, lo:hi], preferred_element_type=F32)

    def conv_finish(lo, hi, b_gate, c_gate, hc):
        u = c_gate * hc
        w0, w1, w2 = (convw_ref[layer, t:t + 1, lo:hi] for t in range(CONV_WIDTH))
        y = w0 * pltpu.roll(u, 2, 0) + w1 * pltpu.roll(u, 1, 0) + w2 * u
        head = jnp.concatenate([ubuf[:, lo:hi], u[0:SUBLANES, :]], axis=0)
        y_head = (w0 * head[SUBLANES - 2:2 * SUBLANES - 2, :] + w1 * head[SUBLANES - 1:2 * SUBLANES - 1, :]
                  + w2 * u[0:SUBLANES, :])
        ubuf[:, lo:hi] = u[tm - SUBLANES:tm, :]
        cbuf[:, lo:hi] = b_gate * jnp.concatenate([y_head, y[SUBLANES:, :]], axis=0)

    conv_steps = []
    for lo in range(0, c3, CONV_CHUNK):
        parts = []
        for base in (0, c3, 2 * c3):
            conv_steps.append(lambda lo=lo, base=base, parts=parts: parts.append(
                in_proj(base + lo, base + lo + CONV_CHUNK)))
        conv_steps.append(lambda lo=lo, parts=parts: conv_finish(lo, lo + CONV_CHUNK, *parts))
    proj_c = []
    conv_steps.append(lambda: proj_c.append(
        jnp.dot((cbuf[...] * cog_ref[this_layer, :]).astype(BF16), wo_ref[0:c3, :], preferred_element_type=F32)))

    kv = in_proj(KV_OFF + ATTN_WIDTH, IN_COLS)
    q_t = lax.dot_general(wqt[...], h, (((1,), (1,)), ((), ())), preferred_element_type=F32)
    lane = lax.broadcasted_iota(jnp.int32, (1, LANES), 1)
    lo_half = lane < HEAD_DIM

    k_sq = kv[:, 0:LANES] * kv[:, 0:LANES]
    ms_lo = jnp.sum(jnp.where(lo_half, k_sq, 0.0), axis=-1, keepdims=True)
    ms_hi = jnp.sum(jnp.where(lo_half, 0.0, k_sq), axis=-1, keepdims=True)
    kn = kv[:, 0:LANES] * lax.rsqrt(jnp.where(lo_half, ms_lo, ms_hi) * (1.0 / HEAD_DIM) + EPS) * kg_ref[this_layer, :]
    k_a0 = jnp.where(lo_half, kn, 0.0)
    k_b1 = jnp.where(lo_half, 0.0, kn)
    kk[0, BLOCK:BLOCK + tm, :] = k_a0.astype(BF16)
    kk[1, BLOCK:BLOCK + tm, :] = pltpu.roll(k_a0, HEAD_DIM, 1).astype(BF16)
    kk[2, BLOCK:BLOCK + tm, :] = pltpu.roll(k_b1, HEAD_DIM, 1).astype(BF16)
    kk[3, BLOCK:BLOCK + tm, :] = k_b1.astype(BF16)

    for hd in range(N_Q_HEADS):
        c, hf = divmod(hd, 2)
        g, r = divmod(c, COLS_PER_KV)
        t = q_t[hd * HEAD_DIM:(hd + 1) * HEAD_DIM, :]
        inv = lax.rsqrt(jnp.mean(t * t, axis=0, keepdims=True) + EPS)
        for b in range(nblk):
            blk = slice(b * BLOCK, (b + 1) * BLOCK)
            qq[g, b, hf * HEAD_DIM:(hf + 1) * HEAD_DIM, r * BLOCK:(r + 1) * BLOCK] = (
                t[:, blk] * inv[:, blk]).astype(BF16)

    v_t = kv[:, LANES:2 * LANES].T
    ones_row = (lax.broadcasted_iota(jnp.int32, (HEAD_DIM, tm), 0) == 0).astype(F32)
    for g in range(N_KV_HEADS):
        v_g = v_t[g * HEAD_DIM:(g + 1) * HEAD_DIM, :]
        vt[2 * g, :, BLOCK:BLOCK + tm] = jnp.concatenate([v_g, ones_row], axis=0).astype(BF16)
        vt[2 * g + 1, :, BLOCK:BLOCK + tm] = jnp.concatenate([ones_row, v_g], axis=0).astype(BF16)

    pairs = [(b, g) for b in range(nblk) for g in range(N_KV_HEADS)]

    key_i = lax.broadcasted_iota(jnp.int32, (2 * BLOCK, COLS_PER_KV * BLOCK), 0)
    qry_i = lax.broadcasted_iota(jnp.int32, (2 * BLOCK, COLS_PER_KV * BLOCK), 1) & (BLOCK - 1)
    band = (key_i > qry_i) & (key_i <= qry_i + BLOCK)
    first_band = band & (key_i >= jnp.where(j == 0, BLOCK, 0))
    cap = jnp.where(band, jnp.inf, NEG_INF)
    first_col = lax.broadcasted_iota(jnp.int32, (1, COLS_PER_KV * BLOCK), 1) < BLOCK
    first_cap = jnp.where(first_band, jnp.inf, NEG_INF)

    def scores(i):
        b, g = pairs[i]
        for half in range(2):
            keys = kk[2 * g + half, b * BLOCK:(b + 2) * BLOCK, :]
            sbuf[2 * i + half] = jnp.dot(keys, qq[g, b], preferred_element_type=F32)

    sink_term = {}

    def softmax(i):
        b, g = pairs[i]
        for half in range(2):
            s = jnp.minimum(sbuf[2 * i + half], first_cap if b == 0 else cap)
            sink = jnp.where(first_col, sink_ref[layer, 4 * g + half] * LOG2E,
                             sink_ref[layer, 4 * g + 2 + half] * LOG2E)
            m = jnp.maximum(jnp.max(s, axis=0, keepdims=True), sink)
            sink_term[2 * i + half] = jnp.exp2(sink - m)
            pbuf[2 * i + half] = jnp.exp2(s - m).astype(BF16)

    def weighted_values(i):
        b, g = pairs[i]
        keys = slice(b * BLOCK, (b + 2) * BLOCK)
        o_lo = jnp.dot(vt[2 * g, :, keys], pbuf[2 * i], preferred_element_type=F32)
        o_hi = jnp.dot(vt[2 * g + 1, :, keys], pbuf[2 * i + 1], preferred_element_type=F32)
        inv_lo = 1.0 / (o_lo[HEAD_DIM:HEAD_DIM + 1, :] + sink_term[2 * i])
        inv_hi = 1.0 / (o_hi[0:1, :] + sink_term[2 * i + 1])
        o_t = jnp.concatenate([o_lo[0:HEAD_DIM, :] * inv_lo, o_hi[HEAD_DIM:2 * HEAD_DIM, :] * inv_hi], axis=0)
        for r in range(COLS_PER_KV):
            c = COLS_PER_KV * g + r
            abuf[b * BLOCK:(b + 1) * BLOCK, c * LANES:(c + 1) * LANES] = o_t[:, r * BLOCK:(r + 1) * BLOCK].T

    n_iter, n_conv = len(pairs) + 2, len(conv_steps)
    for i in range(n_iter):
        while i >= FILL_START and len(conv_steps) * (n_iter - FILL_START) > (n_iter - 1 - i) * n_conv:
            conv_steps.pop(0)()
        if i < len(pairs):
            scores(i)
        if 0 <= i - 1 < len(pairs):
            softmax(i - 1)
        if i - 2 >= 0:
            weighted_values(i - 2)

    kk[:, 0:BLOCK, :] = kk[:, tm:tm + BLOCK, :]
    vt[:, :, 0:BLOCK] = vt[:, :, tm:tm + BLOCK]

    mix_a = _rms(abuf[...], aog_ref[this_layer, :]).astype(BF16)
    conv_out = cbuf[...]
    inv_c = lax.rsqrt(jnp.mean(conv_out * conv_out, axis=-1, keepdims=True) + EPS)
    out_ref[0] = (x_ref[0] + proj_c[0] * inv_c
                  + jnp.dot(mix_a, wo_ref[c3:c3 + ATTN_WIDTH, :], preferred_element_type=F32))


def _ffn_kernel(x_ref, g_ref, wg_ref, wu_ref, wd_ref, out_ref, *, layer):
    n_sub = x_ref.shape[0] // FF_ROWS
    chunks = [(lo, min(lo + FF_CHUNK, D_FF)) for lo in range(0, D_FF, FF_CHUNK)]
    items = [(r, c) for r in range(n_sub) for c in range(len(chunks))]

    def rows(r):
        return slice(r * FF_ROWS, (r + 1) * FF_ROWS)

    def normed(r):
        x = x_ref[rows(r), :]
        inv_rms = lax.rsqrt(jnp.mean(x * x, axis=-1, keepdims=True) + EPS)
        return (x * g_ref[layer:layer + 1, :]).astype(BF16), inv_rms

    def gate_up(h, c):
        lo, hi = chunks[c]
        xg, inv_rms = h
        return (jnp.dot(xg, wg_ref[:, lo:hi], preferred_element_type=F32) * inv_rms,
                jnp.dot(xg, wu_ref[:, lo:hi], preferred_element_type=F32) * inv_rms)

    h = {0: normed(0)}
    nxt = gate_up(h[0], 0)
    acc = None
    for k, (r, c) in enumerate(items):
        gate, up = nxt
        if c == 0:
            acc = x_ref[rows(r), :]
            if r + 1 < n_sub:
                h[r + 1] = normed(r + 1)
        if k + 1 < len(items):
            r1, c1 = items[k + 1]
            nxt = gate_up(h[r1], c1)
        act = (gate * jax.nn.sigmoid(gate) * up).astype(BF16)
        lo, hi = chunks[c]
        acc = acc + jnp.dot(act, wd_ref[lo:hi, :], preferred_element_type=F32)
        if c == len(chunks) - 1:
            out_ref[rows(r), :] = acc


def _resident(shape, index_map):
    return pl.BlockSpec(shape, index_map, pipeline_mode=pl.Buffered(1))


def _mixer(x, g1, w_in, conv_w, kg2, sinks, cog, aog, w_o, to_cast, layer, tm):
    b, seq, d = x.shape
    depth = g1.shape[0]
    nj = seq // tm
    steps = b * nj
    lsel3 = lambda bi, ji: (layer, 0, 0)
    top2 = lambda bi, ji: (0, 0)
    vec = lambda n: _resident((depth, n), top2)
    n_units = (tm // BLOCK) * N_KV_HEADS * 2
    cast_in, cast_out, cast_shapes = [], [], []
    for w, wl in to_cast:
        rows, cols = w.shape[1] // steps, w.shape[2]
        assert rows * steps == w.shape[1] and rows % (2 * SUBLANES) == 0
        cast_in.append(pl.BlockSpec((None, rows, cols), lambda bi, ji, wl=wl: (wl, bi * nj + ji, 0)))
        cast_out.append(pl.BlockSpec((rows, cols), lambda bi, ji: (bi * nj + ji, 0)))
        cast_shapes.append(jax.ShapeDtypeStruct(w.shape[1:], BF16))
    f32_weights = w_in.dtype != BF16
    if f32_weights:
        w_in_spec, w_o_spec = _resident((None, d, IN_COLS), lsel3), _resident((None, CONV_CHANNELS + ATTN_WIDTH, d), lsel3)
        own_bf16 = [pltpu.VMEM((d, IN_COLS), BF16), pltpu.VMEM((CONV_CHANNELS + ATTN_WIDTH, d), BF16)]
    else:
        w_in_spec, w_o_spec = _resident((d, IN_COLS), top2), _resident((CONV_CHANNELS + ATTN_WIDTH, d), top2)
        own_bf16 = []
    outs = pl.pallas_call(
        functools.partial(_mixer_kernel, tm=tm, n_cast=len(to_cast), layer=layer, f32_weights=f32_weights),
        name=f"mixer_l{layer}",
        grid=(b, nj),
        in_specs=[
            pl.BlockSpec((1, tm, d), lambda bi, ji: (bi, ji, 0)),
            vec(d),
            w_in_spec,
            _resident(conv_w.shape, lambda bi, ji: (0, 0, 0)),
            vec(LANES),
            pl.BlockSpec(memory_space=pltpu.SMEM),
            vec(CONV_CHANNELS),
            vec(ATTN_WIDTH),
            w_o_spec,
        ] + cast_in,
        out_specs=[pl.BlockSpec((1, tm, d), lambda bi, ji: (bi, ji, 0))] + cast_out,
        out_shape=[jax.ShapeDtypeStruct(x.shape, x.dtype)] + cast_shapes,
        scratch_shapes=[
            pltpu.VMEM((SUBLANES, CONV_CHANNELS), F32),
            pltpu.VMEM((N_KV_HEADS, tm // BLOCK, LANES, COLS_PER_KV * BLOCK), BF16),
            pltpu.VMEM((4, tm + BLOCK, LANES), BF16),
            pltpu.VMEM((4, LANES, tm + BLOCK), BF16),
            pltpu.VMEM((n_units, 2 * BLOCK, COLS_PER_KV * BLOCK), F32),
            pltpu.VMEM((n_units, 2 * BLOCK, COLS_PER_KV * BLOCK), BF16),
            pltpu.VMEM((tm, ATTN_WIDTH), F32),
            pltpu.VMEM((tm, CONV_CHANNELS), F32),
            pltpu.VMEM((ATTN_WIDTH, d), BF16),
        ] + own_bf16,
        compiler_params=pltpu.CompilerParams(
            dimension_semantics=("arbitrary", "arbitrary"),
            vmem_limit_bytes=VMEM_LIMIT_BYTES,
        ),
    )(x, g1, w_in, conv_w, kg2, sinks, cog, aog, w_o, *[w for w, _ in to_cast])
    return outs[0], outs[1:]


def _ffn(x2d, g, w_gate, w_up, w_down, layer, tm):
    n, d = x2d.shape
    top2 = lambda i: (0, 0)
    return pl.pallas_call(
        functools.partial(_ffn_kernel, layer=layer),
        name=f"ffn_l{layer}",
        grid=(n // tm,),
        in_specs=[
            pl.BlockSpec((tm, d), lambda i: (i, 0)),
            _resident(g.shape, top2),
            _resident((d, D_FF), top2),
            _resident((d, D_FF), top2),
            _resident((D_FF, d), top2),
        ],
        out_specs=pl.BlockSpec((tm, d), lambda i: (i, 0)),
        out_shape=jax.ShapeDtypeStruct(x2d.shape, x2d.dtype),
        compiler_params=pltpu.CompilerParams(
            dimension_semantics=("arbitrary",),
            vmem_limit_bytes=VMEM_LIMIT_BYTES,
        ),
    )(x2d, g, w_gate, w_up, w_down)


def kernel(x, norm1_g, w_in, conv_w, q_norm_g, k_norm_g, sinks, conv_out_g, attn_out_g, w_o, norm2_g,
           w_gate, w_up, w_down):
    b, seq, d = x.shape
    depth = w_in.shape[0]
    assert seq % TM_MIXER == 0 and (b * seq) % TM_FFN == 0 and TM_MIXER % BLOCK == 0

    kg2 = jnp.tile(k_norm_g * q_norm_g * (HEAD_DIM ** -0.5 * LOG2E), (1, 2))
    w_in_b, w_o_b = w_in, w_o

    for l in range(depth):
        to_cast = [(w_gate, l), (w_up, l), (w_down, l)]
        if l + 1 < depth:
            to_cast += [(w_in, l + 1), (w_o, l + 1)]
        x, cast = _mixer(x, norm1_g, w_in_b, conv_w, kg2, sinks, conv_out_g, attn_out_g, w_o_b, to_cast, l,
                         TM_MIXER)
        w_gate_b, w_up_b, w_down_b = cast[:3]
        if l + 1 < depth:
            w_in_b, w_o_b = cast[3:]
        x = _ffn(x.reshape(b * seq, d), norm2_g, w_gate_b, w_up_b, w_down_b, l, TM_FFN).reshape(b, seq, d)
    return x
```

```python
import functools

import jax
import jax.numpy as jnp
from jax import lax
from jax.experimental import pallas as pl
from jax.experimental.pallas import tpu as pltpu

D_MODEL = 1024
CONV_CHANNELS = 512
CONV_WIDTH = 3
N_Q_HEADS = 8
N_KV_HEADS = 2
HEAD_DIM = 64
ATTN_WIDTH = N_Q_HEADS * HEAD_DIM
BLOCK = 128
IN_COLS = 3 * CONV_CHANNELS + (N_Q_HEADS + 2 * N_KV_HEADS) * HEAD_DIM
D_FF = 2816
EPS = 1e-6
NEG_INF = -1e30

LANES = 128
SUBLANES = 8
Q_COLS = ATTN_WIDTH // LANES
COLS_PER_KV = Q_COLS // N_KV_HEADS
LOG2E = 1.4426950408889634
KV_OFF = 3 * CONV_CHANNELS
VMEM_LIMIT_BYTES = 60 * 1024 * 1024
FFN_VMEM_LIMIT_BYTES = 48 * 1024 * 1024

TM_MIXER = 1024
CONV_CHUNK = 256
N_MIXER_IN = 9
TM_FFN = 1024
FF_ROWS = 256
FF_CHUNK = 1024

F32 = jnp.float32
BF16 = jnp.bfloat16


def _rms(x, g):
    ms = jnp.mean(x * x, axis=-1, keepdims=True)
    return x * lax.rsqrt(ms + EPS) * g


def _mixer_kernel(*refs, tm, n_cast, layer, f32_weights):
    (x_ref, g1_ref, win_ref, convw_ref, kg_ref, sink_ref, cog_ref, aog_ref, wo_ref) = refs[:N_MIXER_IN]
    cast_src = refs[N_MIXER_IN:N_MIXER_IN + n_cast]
    out_ref = refs[N_MIXER_IN + n_cast]
    cast_dst = refs[N_MIXER_IN + n_cast + 1:N_MIXER_IN + 2 * n_cast + 1]
    ubuf, qq, kk, vt, sbuf, pbuf, abuf, cbuf, wqt, *own_bf16 = refs[N_MIXER_IN + 2 * n_cast + 1:]
    first_step = (pl.program_id(0) == 0) & (pl.program_id(1) == 0)
    if f32_weights:
        @pl.when(first_step)
        def _():
            for src, dst in zip((win_ref, wo_ref), own_bf16):
                dst[...] = src[...].astype(BF16)
        win_ref, wo_ref = own_bf16
    j = pl.program_id(1)
    this_layer = slice(layer, layer + 1)

    for src, dst in zip(cast_src, cast_dst):
        dst[...] = src[...].astype(BF16)
    c3 = CONV_CHANNELS
    nblk = tm // BLOCK

    @pl.when(first_step)
    def _():
        wqt[...] = win_ref[:, KV_OFF:KV_OFF + ATTN_WIDTH].astype(F32).T.astype(BF16)

    @pl.when(j == 0)
    def _():
        ubuf[0:SUBLANES, :] = jnp.zeros((SUBLANES, c3), F32)
        kk[:, 0:BLOCK, :] = jnp.zeros((4, BLOCK, LANES), BF16)
        vt[:, :, 0:BLOCK] = jnp.zeros((4, LANES, BLOCK), BF16)

    h = _rms(x_ref[0], g1_ref[this_layer, :]).astype(BF16)

    def in_proj(lo, hi):
        return jnp.dot(h, win_ref[:, lo:hi], preferred_element_type=F32)

    def conv_finish(lo, hi, b_gate, c_gate, hc):
        u = c_gate * hc
        w0, w1, w2 = (convw_ref[layer, t:t + 1, lo:hi] for t in range(CONV_WIDTH))
        y = w0 * pltpu.roll(u, 2, 0) + w1 * pltpu.roll(u, 1, 0) + w2 * u
        head = jnp.concatenate([ubuf[:, lo:hi], u[0:SUBLANES, :]], axis=0)
        y_head = (w0 * head[SUBLANES - 2:2 * SUBLANES - 2, :] + w1 * head[SUBLANES - 1:2 * SUBLANES - 1, :]
                  + w2 * u[0:SUBLANES, :])
        ubuf[:, lo:hi] = u[tm - SUBLANES:tm, :]
        cbuf[:, lo:hi] = b_gate * jnp.concatenate([y_head, y[SUBLANES:, :]], axis=0)

    conv_steps = []
    for lo in range(0, c3, CONV_CHUNK):
        parts = []
        for base in (0, c3, 2 * c3):
            conv_steps.append(lambda lo=lo, base=base, parts=parts: parts.append(
                in_proj(base + lo, base + lo + CONV_CHUNK)))
        conv_steps.append(lambda lo=lo, parts=parts: conv_finish(lo, lo + CONV_CHUNK, *parts))
    proj_c = []
    conv_steps.append(lambda: proj_c.append(
        jnp.dot((cbuf[...] * cog_ref[this_layer, :]).astype(BF16), wo_ref[0:c3, :], preferred_element_type=F32)))

    kv = in_proj(KV_OFF + ATTN_WIDTH, IN_COLS)
    q_t = lax.dot_general(wqt[...], h, (((1,), (1,)), ((), ())), preferred_element_type=F32)
    lane = lax.broadcasted_iota(jnp.int32, (1, LANES), 1)
    lo_half = lane < HEAD_DIM

    k_sq = kv[:, 0:LANES] * kv[:, 0:LANES]
    ms_lo = jnp.sum(jnp.where(lo_half, k_sq, 0.0), axis=-1, keepdims=True)
    ms_hi = jnp.sum(jnp.where(lo_half, 0.0, k_sq), axis=-1, keepdims=True)
    kn = kv[:, 0:LANES] * lax.rsqrt(jnp.where(lo_half, ms_lo, ms_hi) * (1.0 / HEAD_DIM) + EPS) * kg_ref[this_layer, :]
    k_a0 = jnp.where(lo_half, kn, 0.0)
    k_b1 = jnp.where(lo_half, 0.0, kn)
    kk[0, BLOCK:BLOCK + tm, :] = k_a0.astype(BF16)
    kk[1, BLOCK:BLOCK + tm, :] = pltpu.roll(k_a0, HEAD_DIM, 1).astype(BF16)
    kk[2, BLOCK:BLOCK + tm, :] = pltpu.roll(k_b1, HEAD_DIM, 1).astype(BF16)
    kk[3, BLOCK:BLOCK + tm, :] = k_b1.astype(BF16)

    for hd in range(N_Q_HEADS):
        c, hf = divmod(hd, 2)
        g, r = divmod(c, COLS_PER_KV)
        t = q_t[hd * HEAD_DIM:(hd + 1) * HEAD_DIM, :]
        inv = lax.rsqrt(jnp.mean(t * t, axis=0, keepdims=True) + EPS)
        for b in range(nblk):
            blk = slice(b * BLOCK, (b + 1) * BLOCK)
            qq[g, b, hf * HEAD_DIM:(hf + 1) * HEAD_DIM, r * BLOCK:(r + 1) * BLOCK] = (
                t[:, blk] * inv[:, blk]).astype(BF16)

    v_t = kv[:, LANES:2 * LANES].T
    ones_row = (lax.broadcasted_iota(jnp.int32, (HEAD_DIM, tm), 0) == 0).astype(F32)
    for g in range(N_KV_HEADS):
        v_g = v_t[g * HEAD_DIM:(g + 1) * HEAD_DIM, :]
        vt[2 * g, :, BLOCK:BLOCK + tm] = jnp.concatenate([v_g, ones_row], axis=0).astype(BF16)
        vt[2 * g + 1, :, BLOCK:BLOCK + tm] = jnp.concatenate([ones_row, v_g], axis=0).astype(BF16)

    pairs = [(b, g) for b in range(nblk) for g in range(N_KV_HEADS)]

    key_i = lax.broadcasted_iota(jnp.int32, (2 * BLOCK, COLS_PER_KV * BLOCK), 0)
    qry_i = lax.broadcasted_iota(jnp.int32, (2 * BLOCK, COLS_PER_KV * BLOCK), 1) & (BLOCK - 1)
    band = (key_i > qry_i) & (key_i <= qry_i + BLOCK)
    first_band = band & (key_i >= jnp.where(j == 0, BLOCK, 0))
    cap = jnp.where(band, jnp.inf, NEG_INF)
    first_col = lax.broadcasted_iota(jnp.int32, (1, COLS_PER_KV * BLOCK), 1) < BLOCK
    first_cap = jnp.where(first_band, jnp.inf, NEG_INF)

    def scores(i):
        b, g = pairs[i]
        for half in range(2):
            keys = kk[2 * g + half, b * BLOCK:(b + 2) * BLOCK, :]
            sbuf[2 * i + half] = jnp.dot(keys, qq[g, b], preferred_element_type=F32)

    sink_term = {}

    def softmax(i):
        b, g = pairs[i]
        for half in range(2):
            s = jnp.minimum(sbuf[2 * i + half], first_cap if b == 0 else cap)
            sink = jnp.where(first_col, sink_ref[layer, 4 * g + half] * LOG2E,
                             sink_ref[layer, 4 * g + 2 + half] * LOG2E)
            m = jnp.maximum(jnp.max(s, axis=0, keepdims=True), sink)
            sink_term[2 * i + half] = jnp.exp2(sink - m)
            pbuf[2 * i + half] = jnp.exp2(s - m).astype(BF16)

    def weighted_values(i):
        b, g = pairs[i]
        keys = slice(b * BLOCK, (b + 2) * BLOCK)
        o_lo = jnp.dot(vt[2 * g, :, keys], pbuf[2 * i], preferred_element_type=F32)
        o_hi = jnp.dot(vt[2 * g + 1, :, keys], pbuf[2 * i + 1], preferred_element_type=F32)
        inv_lo = 1.0 / (o_lo[HEAD_DIM:HEAD_DIM + 1, :] + sink_term[2 * i])
        inv_hi = 1.0 / (o_hi[0:1, :] + sink_term[2 * i + 1])
        o_t = jnp.concatenate([o_lo[0:HEAD_DIM, :] * inv_lo, o_hi[HEAD_DIM:2 * HEAD_DIM, :] * inv_hi], axis=0)
        for r in range(COLS_PER_KV):
            c = COLS_PER_KV * g + r
            abuf[b * BLOCK:(b + 1) * BLOCK, c * LANES:(c + 1) * LANES] = o_t[:, r * BLOCK:(r + 1) * BLOCK].T

    n_iter, n_conv = len(pairs) + 2, len(conv_steps)
    for i in range(n_iter):
        while len(conv_steps) * n_iter > (n_iter - 1 - i) * n_conv:
            conv_steps.pop(0)()
        if i < len(pairs):
            scores(i)
        if 0 <= i - 1 < len(pairs):
            softmax(i - 1)
        if i - 2 >= 0:
            weighted_values(i - 2)

    kk[:, 0:BLOCK, :] = kk[:, tm:tm + BLOCK, :]
    vt[:, :, 0:BLOCK] = vt[:, :, tm:tm + BLOCK]

    mix_a = _rms(abuf[...], aog_ref[this_layer, :]).astype(BF16)
    conv_out = cbuf[...]
    inv_c = lax.rsqrt(jnp.mean(conv_out * conv_out, axis=-1, keepdims=True) + EPS)
    out_ref[0] = (x_ref[0] + proj_c[0] * inv_c
                  + jnp.dot(mix_a, wo_ref[c3:c3 + ATTN_WIDTH, :], preferred_element_type=F32))


def _ffn_kernel(x_ref, g_ref, wg_ref, wu_ref, wd_ref, out_ref, *, layer):
    n_sub = x_ref.shape[0] // FF_ROWS
    chunks = [(lo, min(lo + FF_CHUNK, D_FF)) for lo in range(0, D_FF, FF_CHUNK)]
    items = [(r, c) for r in range(n_sub) for c in range(len(chunks))]

    def rows(r):
        return slice(r * FF_ROWS, (r + 1) * FF_ROWS)

    def normed(r):
        x = x_ref[rows(r), :]
        inv_rms = lax.rsqrt(jnp.mean(x * x, axis=-1, keepdims=True) + EPS)
        return (x * g_ref[layer:layer + 1, :]).astype(BF16), inv_rms

    def gate_up(h, c):
        lo, hi = chunks[c]
        xg, inv_rms = h
        return (jnp.dot(xg, wg_ref[:, lo:hi], preferred_element_type=F32) * inv_rms,
                jnp.dot(xg, wu_ref[:, lo:hi], preferred_element_type=F32) * inv_rms)

    h = {0: normed(0)}
    nxt = gate_up(h[0], 0)
    acc = None
    for k, (r, c) in enumerate(items):
        gate, up = nxt
        if c == 0:
            acc = x_ref[rows(r), :]
            if r + 1 < n_sub:
                h[r + 1] = normed(r + 1)
        if k + 1 < len(items):
            r1, c1 = items[k + 1]
            nxt = gate_up(h[r1], c1)
        act = (gate * jax.nn.sigmoid(gate) * up).astype(BF16)
        lo, hi = chunks[c]
        acc = acc + jnp.dot(act, wd_ref[lo:hi, :], preferred_element_type=F32)
        if c == len(chunks) - 1:
            out_ref[rows(r), :] = acc


def _resident(shape, index_map):
    return pl.BlockSpec(shape, index_map, pipeline_mode=pl.Buffered(1))


def _mixer(x, g1, w_in, conv_w, kg2, sinks, cog, aog, w_o, to_cast, layer, tm):
    b, seq, d = x.shape
    depth = g1.shape[0]
    nj = seq // tm
    steps = b * nj
    lsel3 = lambda bi, ji: (layer, 0, 0)
    top2 = lambda bi, ji: (0, 0)
    vec = lambda n: _resident((depth, n), top2)
    n_units = (tm // BLOCK) * N_KV_HEADS * 2
    cast_in, cast_out, cast_shapes = [], [], []
    for w, wl in to_cast:
        rows, cols = w.shape[1] // steps, w.shape[2]
        assert rows * steps == w.shape[1] and rows % (2 * SUBLANES) == 0
        cast_in.append(pl.BlockSpec((None, rows, cols), lambda bi, ji, wl=wl: (wl, bi * nj + ji, 0)))
        cast_out.append(pl.BlockSpec((rows, cols), lambda bi, ji: (bi * nj + ji, 0)))
        cast_shapes.append(jax.ShapeDtypeStruct(w.shape[1:], BF16))
    f32_weights = w_in.dtype != BF16
    if f32_weights:
        w_in_spec, w_o_spec = _resident((None, d, IN_COLS), lsel3), _resident((None, CONV_CHANNELS + ATTN_WIDTH, d), lsel3)
        own_bf16 = [pltpu.VMEM((d, IN_COLS), BF16), pltpu.VMEM((CONV_CHANNELS + ATTN_WIDTH, d), BF16)]
    else:
        w_in_spec, w_o_spec = _resident((d, IN_COLS), top2), _resident((CONV_CHANNELS + ATTN_WIDTH, d), top2)
        own_bf16 = []
    outs = pl.pallas_call(
        functools.partial(_mixer_kernel, tm=tm, n_cast=len(to_cast), layer=layer, f32_weights=f32_weights),
        name=f"mixer_l{layer}",
        grid=(b, nj),
        in_specs=[
            pl.BlockSpec((1, tm, d), lambda bi, ji: (bi, ji, 0)),
            vec(d),
            w_in_spec,
            _resident(conv_w.shape, lambda bi, ji: (0, 0, 0)),
            vec(LANES),
            pl.BlockSpec(memory_space=pltpu.SMEM),
            vec(CONV_CHANNELS),
            vec(ATTN_WIDTH),
            w_o_spec,
        ] + cast_in,
        out_specs=[pl.BlockSpec((1, tm, d), lambda bi, ji: (bi, ji, 0))] + cast_out,
        out_shape=[jax.ShapeDtypeStruct(x.shape, x.dtype)] + cast_shapes,
        scratch_shapes=[
            pltpu.VMEM((SUBLANES, CONV_CHANNELS), F32),
            pltpu.VMEM((N_KV_HEADS, tm // BLOCK, LANES, COLS_PER_KV * BLOCK), BF16),
            pltpu.VMEM((4, tm + BLOCK, LANES), BF16),
            pltpu.VMEM((4, LANES, tm + BLOCK), BF16),
            pltpu.VMEM((n_units, 2 * BLOCK, COLS_PER_KV * BLOCK), F32),
            pltpu.VMEM((n_units, 2 * BLOCK, COLS_PER_KV * BLOCK), BF16),
            pltpu.VMEM((tm, ATTN_WIDTH), F32),
            pltpu.VMEM((tm, CONV_CHANNELS), F32),
            pltpu.VMEM((ATTN_WIDTH, d), BF16),
        ] + own_bf16,
        compiler_params=pltpu.CompilerParams(
            dimension_semantics=("arbitrary", "arbitrary"),
            vmem_limit_bytes=VMEM_LIMIT_BYTES,
        ),
    )(x, g1, w_in, conv_w, kg2, sinks, cog, aog, w_o, *[w for w, _ in to_cast])
    return outs[0], outs[1:]


def _ffn(x2d, g, w_gate, w_up, w_down, layer, tm):
    n, d = x2d.shape
    top2 = lambda i: (0, 0)
    return pl.pallas_call(
        functools.partial(_ffn_kernel, layer=layer),
        name=f"ffn_l{layer}",
        grid=(n // tm,),
        in_specs=[
            pl.BlockSpec((tm, d), lambda i: (i, 0)),
            _resident(g.shape, top2),
            _resident((d, D_FF), top2),
            _resident((d, D_FF), top2),
            _resident((D_FF, d), top2),
        ],
        out_specs=pl.BlockSpec((tm, d), lambda i: (i, 0)),
        out_shape=jax.ShapeDtypeStruct(x2d.shape, x2d.dtype),
        compiler_params=pltpu.CompilerParams(
            dimension_semantics=("arbitrary",),
            vmem_limit_bytes=FFN_VMEM_LIMIT_BYTES,
        ),
    )(x2d, g, w_gate, w_up, w_down)


def kernel(x, norm1_g, w_in, conv_w, q_norm_g, k_norm_g, sinks, conv_out_g, attn_out_g, w_o, norm2_g,
           w_gate, w_up, w_down):
    b, seq, d = x.shape
    depth = w_in.shape[0]
    assert seq % TM_MIXER == 0 and (b * seq) % TM_FFN == 0 and TM_MIXER % BLOCK == 0

    kg2 = jnp.tile(k_norm_g * q_norm_g * (HEAD_DIM ** -0.5 * LOG2E), (1, 2))
    w_in_b, w_o_b = w_in, w_o

    for l in range(depth):
        to_cast = [(w_gate, l), (w_up, l), (w_down, l)]
        if l + 1 < depth:
            to_cast += [(w_in, l + 1), (w_o, l + 1)]
        x, cast = _mixer(x, norm1_g, w_in_b, conv_w, kg2, sinks, conv_out_g, attn_out_g, w_o_b, to_cast, l,
                         TM_MIXER)
        w_gate_b, w_up_b, w_down_b = cast[:3]
        if l + 1 < depth:
            w_in_b, w_o_b = cast[3:]
        x = _ffn(x.reshape(b * seq, d), norm2_g, w_gate_b, w_up_b, w_down_b, l, TM_FFN).reshape(b, seq, d)
    return x
```

```python
import functools

import jax
import jax.numpy as jnp
from jax import lax
from jax.experimental import pallas as pl
from jax.experimental.pallas import tpu as pltpu

D_MODEL = 1024
CONV_CHANNELS = 512
CONV_WIDTH = 3
N_Q_HEADS = 8
N_KV_HEADS = 2
HEAD_DIM = 64
ATTN_WIDTH = N_Q_HEADS * HEAD_DIM
BLOCK = 128
IN_COLS = 3 * CONV_CHANNELS + (N_Q_HEADS + 2 * N_KV_HEADS) * HEAD_DIM
D_FF = 2816
EPS = 1e-6
NEG_INF = -1e30

LANES = 128
SUBLANES = 8
Q_COLS = ATTN_WIDTH // LANES
COLS_PER_KV = Q_COLS // N_KV_HEADS
LOG2E = 1.4426950408889634
KV_OFF = 3 * CONV_CHANNELS
VMEM_LIMIT_BYTES = 60 * 1024 * 1024
FFN_VMEM_LIMIT_BYTES = 54 * 1024 * 1024

TM_MIXER = 1024
CONV_CHUNK = 256
N_MIXER_IN = 9
TM_FFN = 1024
FF_ROWS = 256
FF_CHUNK = 1024

F32 = jnp.float32
BF16 = jnp.bfloat16


def _rms(x, g):
    ms = jnp.mean(x * x, axis=-1, keepdims=True)
    return x * lax.rsqrt(ms + EPS) * g


def _mixer_kernel(*refs, tm, n_cast, layer, f32_weights):
    (x_ref, g1_ref, win_ref, convw_ref, kg_ref, sink_ref, cog_ref, aog_ref, wo_ref) = refs[:N_MIXER_IN]
    cast_src = refs[N_MIXER_IN:N_MIXER_IN + n_cast]
    out_ref = refs[N_MIXER_IN + n_cast]
    cast_dst = refs[N_MIXER_IN + n_cast + 1:N_MIXER_IN + 2 * n_cast + 1]
    ubuf, qq, kk, vt, sbuf, pbuf, abuf, cbuf, wqt, *own_bf16 = refs[N_MIXER_IN + 2 * n_cast + 1:]
    first_step = (pl.program_id(0) == 0) & (pl.program_id(1) == 0)
    if f32_weights:
        @pl.when(first_step)
        def _():
            for src, dst in zip((win_ref, wo_ref), own_bf16):
                dst[...] = src[...].astype(BF16)
        win_ref, wo_ref = own_bf16
    j = pl.program_id(1)
    this_layer = slice(layer, layer + 1)

    for src, dst in zip(cast_src, cast_dst):
        dst[...] = src[...].astype(BF16)
    c3 = CONV_CHANNELS
    nblk = tm // BLOCK

    @pl.when(first_step)
    def _():
        wqt[...] = win_ref[:, KV_OFF:KV_OFF + ATTN_WIDTH].astype(F32).T.astype(BF16)

    @pl.when(j == 0)
    def _():
        ubuf[0:SUBLANES, :] = jnp.zeros((SUBLANES, c3), F32)
        kk[:, 0:BLOCK, :] = jnp.zeros((4, BLOCK, LANES), BF16)
        vt[:, :, 0:BLOCK] = jnp.zeros((4, LANES, BLOCK), BF16)

    h = _rms(x_ref[0], g1_ref[this_layer, :]).astype(BF16)

    def in_proj(lo, hi):
        return jnp.dot(h, win_ref[:, lo:hi], preferred_element_type=F32)

    def conv_finish(lo, hi, b_gate, c_gate, hc):
        u = c_gate * hc
        w0, w1, w2 = (convw_ref[layer, t:t + 1, lo:hi] for t in range(CONV_WIDTH))
        y = w0 * pltpu.roll(u, 2, 0) + w1 * pltpu.roll(u, 1, 0) + w2 * u
        head = jnp.concatenate([ubuf[:, lo:hi], u[0:SUBLANES, :]], axis=0)
        y_head = (w0 * head[SUBLANES - 2:2 * SUBLANES - 2, :] + w1 * head[SUBLANES - 1:2 * SUBLANES - 1, :]
                  + w2 * u[0:SUBLANES, :])
        ubuf[:, lo:hi] = u[tm - SUBLANES:tm, :]
        cbuf[:, lo:hi] = b_gate * jnp.concatenate([y_head, y[SUBLANES:, :]], axis=0)

    conv_steps = []
    for lo in range(0, c3, CONV_CHUNK):
        parts = []
        for base in (0, c3, 2 * c3):
            conv_steps.append(lambda lo=lo, base=base, parts=parts: parts.append(
                in_proj(base + lo, base + lo + CONV_CHUNK)))
        conv_steps.append(lambda lo=lo, parts=parts: conv_finish(lo, lo + CONV_CHUNK, *parts))
    proj_c = []
    conv_steps.append(lambda: proj_c.append(
        jnp.dot((cbuf[...] * cog_ref[this_layer, :]).astype(BF16), wo_ref[0:c3, :], preferred_element_type=F32)))

    kv = in_proj(KV_OFF + ATTN_WIDTH, IN_COLS)
    q_t = lax.dot_general(wqt[...], h, (((1,), (1,)), ((), ())), preferred_element_type=F32)
    lane = lax.broadcasted_iota(jnp.int32, (1, LANES), 1)
    lo_half = lane < HEAD_DIM

    k_sq = kv[:, 0:LANES] * kv[:, 0:LANES]
    ms_lo = jnp.sum(jnp.where(lo_half, k_sq, 0.0), axis=-1, keepdims=True)
    ms_hi = jnp.sum(jnp.where(lo_half, 0.0, k_sq), axis=-1, keepdims=True)
    kn = kv[:, 0:LANES] * lax.rsqrt(jnp.where(lo_half, ms_lo, ms_hi) * (1.0 / HEAD_DIM) + EPS) * kg_ref[this_layer, :]
    k_a0 = jnp.where(lo_half, kn, 0.0)
    k_b1 = jnp.where(lo_half, 0.0, kn)
    kk[0, BLOCK:BLOCK + tm, :] = k_a0.astype(BF16)
    kk[1, BLOCK:BLOCK + tm, :] = pltpu.roll(k_a0, HEAD_DIM, 1).astype(BF16)
    kk[2, BLOCK:BLOCK + tm, :] = pltpu.roll(k_b1, HEAD_DIM, 1).astype(BF16)
    kk[3, BLOCK:BLOCK + tm, :] = k_b1.astype(BF16)

    for hd in range(N_Q_HEADS):
        c, hf = divmod(hd, 2)
        g, r = divmod(c, COLS_PER_KV)
        t = q_t[hd * HEAD_DIM:(hd + 1) * HEAD_DIM, :]
        inv = lax.rsqrt(jnp.mean(t * t, axis=0, keepdims=True) + EPS)
        for b in range(nblk):
            blk = slice(b * BLOCK, (b + 1) * BLOCK)
            qq[g, b, hf * HEAD_DIM:(hf + 1) * HEAD_DIM, r * BLOCK:(r + 1) * BLOCK] = (
                t[:, blk] * inv[:, blk]).astype(BF16)

    v_t = kv[:, LANES:2 * LANES].T
    ones_row = (lax.broadcasted_iota(jnp.int32, (HEAD_DIM, tm), 0) == 0).astype(F32)
    for g in range(N_KV_HEADS):
        v_g = v_t[g * HEAD_DIM:(g + 1) * HEAD_DIM, :]
        vt[2 * g, :, BLOCK:BLOCK + tm] = jnp.concatenate([v_g, ones_row], axis=0).astype(BF16)
        vt[2 * g + 1, :, BLOCK:BLOCK + tm] = jnp.concatenate([ones_row, v_g], axis=0).astype(BF16)

    pairs = [(b, g) for b in range(nblk) for g in range(N_KV_HEADS)]

    key_i = lax.broadcasted_iota(jnp.int32, (2 * BLOCK, COLS_PER_KV * BLOCK), 0)
    qry_i = lax.broadcasted_iota(jnp.int32, (2 * BLOCK, COLS_PER_KV * BLOCK), 1) & (BLOCK - 1)
    band = (key_i > qry_i) & (key_i <= qry_i + BLOCK)
    first_band = band & (key_i >= jnp.where(j == 0, BLOCK, 0))
    cap = jnp.where(band, jnp.inf, NEG_INF)
    first_col = lax.broadcasted_iota(jnp.int32, (1, COLS_PER_KV * BLOCK), 1) < BLOCK
    first_cap = jnp.where(first_band, jnp.inf, NEG_INF)

    def scores(i):
        b, g = pairs[i]
        for half in range(2):
            keys = kk[2 * g + half, b * BLOCK:(b + 2) * BLOCK, :]
            sbuf[2 * i + half] = jnp.dot(keys, qq[g, b], preferred_element_type=F32)

    sink_term = {}

    def softmax(i):
        b, g = pairs[i]
        for half in range(2):
            s = jnp.minimum(sbuf[2 * i + half], first_cap if b == 0 else cap)
            sink = jnp.where(first_col, sink_ref[layer, 4 * g + half] * LOG2E,
                             sink_ref[layer, 4 * g + 2 + half] * LOG2E)
            m = jnp.maximum(jnp.max(s, axis=0, keepdims=True), sink)
            sink_term[2 * i + half] = jnp.exp2(sink - m)
            pbuf[2 * i + half] = jnp.exp2(s - m).astype(BF16)

    def weighted_values(i):
        b, g = pairs[i]
        keys = slice(b * BLOCK, (b + 2) * BLOCK)
        o_lo = jnp.dot(vt[2 * g, :, keys], pbuf[2 * i], preferred_element_type=F32)
        o_hi = jnp.dot(vt[2 * g + 1, :, keys], pbuf[2 * i + 1], preferred_element_type=F32)
        inv_lo = 1.0 / (o_lo[HEAD_DIM:HEAD_DIM + 1, :] + sink_term[2 * i])
        inv_hi = 1.0 / (o_hi[0:1, :] + sink_term[2 * i + 1])
        o_t = jnp.concatenate([o_lo[0:HEAD_DIM, :] * inv_lo, o_hi[HEAD_DIM:2 * HEAD_DIM, :] * inv_hi], axis=0)
        for r in range(COLS_PER_KV):
            c = COLS_PER_KV * g + r
            abuf[b * BLOCK:(b + 1) * BLOCK, c * LANES:(c + 1) * LANES] = o_t[:, r * BLOCK:(r + 1) * BLOCK].T

    n_iter, n_conv = len(pairs) + 2, len(conv_steps)
    for i in range(n_iter):
        while len(conv_steps) * n_iter > (n_iter - 1 - i) * n_conv:
            conv_steps.pop(0)()
        if i < len(pairs):
            scores(i)
        if 0 <= i - 1 < len(pairs):
            softmax(i - 1)
        if i - 2 >= 0:
            weighted_values(i - 2)

    kk[:, 0:BLOCK, :] = kk[:, tm:tm + BLOCK, :]
    vt[:, :, 0:BLOCK] = vt[:, :, tm:tm + BLOCK]

    mix_a = _rms(abuf[...], aog_ref[this_layer, :]).astype(BF16)
    conv_out = cbuf[...]
    inv_c = lax.rsqrt(jnp.mean(conv_out * conv_out, axis=-1, keepdims=True) + EPS)
    out_ref[0] = (x_ref[0] + proj_c[0] * inv_c
                  + jnp.dot(mix_a, wo_ref[c3:c3 + ATTN_WIDTH, :], preferred_element_type=F32))


def _ffn_kernel(x_ref, g_ref, wg_ref, wu_ref, wd_ref, out_ref, *, layer):
    n_sub = x_ref.shape[0] // FF_ROWS
    chunks = [(lo, min(lo + FF_CHUNK, D_FF)) for lo in range(0, D_FF, FF_CHUNK)]
    items = [(r, c) for r in range(n_sub) for c in range(len(chunks))]

    def rows(r):
        return slice(r * FF_ROWS, (r + 1) * FF_ROWS)

    def normed(r):
        x = x_ref[rows(r), :]
        inv_rms = lax.rsqrt(jnp.mean(x * x, axis=-1, keepdims=True) + EPS)
        return (x * g_ref[layer:layer + 1, :]).astype(BF16), inv_rms

    def gate_up(h, c):
        lo, hi = chunks[c]
        xg, inv_rms = h
        return (jnp.dot(xg, wg_ref[:, lo:hi], preferred_element_type=F32) * inv_rms,
                jnp.dot(xg, wu_ref[:, lo:hi], preferred_element_type=F32) * inv_rms)

    h = {0: normed(0)}
    nxt = gate_up(h[0], 0)
    acc = None
    for k, (r, c) in enumerate(items):
        gate, up = nxt
        if c == 0:
            acc = x_ref[rows(r), :]
            if r + 1 < n_sub:
                h[r + 1] = normed(r + 1)
        if k + 1 < len(items):
            r1, c1 = items[k + 1]
            nxt = gate_up(h[r1], c1)
        act = (gate * jax.nn.sigmoid(gate) * up).astype(BF16)
        lo, hi = chunks[c]
        acc = acc + jnp.dot(act, wd_ref[lo:hi, :], preferred_element_type=F32)
        if c == len(chunks) - 1:
            out_ref[rows(r), :] = acc


def _resident(shape, index_map):
    return pl.BlockSpec(shape, index_map, pipeline_mode=pl.Buffered(1))


def _mixer(x, g1, w_in, conv_w, kg2, sinks, cog, aog, w_o, to_cast, layer, tm):
    b, seq, d = x.shape
    depth = g1.shape[0]
    nj = seq // tm
    steps = b * nj
    lsel3 = lambda bi, ji: (layer, 0, 0)
    top2 = lambda bi, ji: (0, 0)
    vec = lambda n: _resident((depth, n), top2)
    n_units = (tm // BLOCK) * N_KV_HEADS * 2
    cast_in, cast_out, cast_shapes = [], [], []
    for w, wl in to_cast:
        rows, cols = w.shape[1] // steps, w.shape[2]
        assert rows * steps == w.shape[1] and rows % (2 * SUBLANES) == 0
        cast_in.append(pl.BlockSpec((None, rows, cols), lambda bi, ji, wl=wl: (wl, bi * nj + ji, 0)))
        cast_out.append(pl.BlockSpec((rows, cols), lambda bi, ji: (bi * nj + ji, 0)))
        cast_shapes.append(jax.ShapeDtypeStruct(w.shape[1:], BF16))
    f32_weights = w_in.dtype != BF16
    if f32_weights:
        w_in_spec, w_o_spec = _resident((None, d, IN_COLS), lsel3), _resident((None, CONV_CHANNELS + ATTN_WIDTH, d), lsel3)
        own_bf16 = [pltpu.VMEM((d, IN_COLS), BF16), pltpu.VMEM((CONV_CHANNELS + ATTN_WIDTH, d), BF16)]
    else:
        w_in_spec, w_o_spec = _resident((d, IN_COLS), top2), _resident((CONV_CHANNELS + ATTN_WIDTH, d), top2)
        own_bf16 = []
    outs = pl.pallas_call(
        functools.partial(_mixer_kernel, tm=tm, n_cast=len(to_cast), layer=layer, f32_weights=f32_weights),
        name=f"mixer_l{layer}",
        grid=(b, nj),
        in_specs=[
            pl.BlockSpec((1, tm, d), lambda bi, ji: (bi, ji, 0)),
            vec(d),
            w_in_spec,
            _resident(conv_w.shape, lambda bi, ji: (0, 0, 0)),
            vec(LANES),
            pl.BlockSpec(memory_space=pltpu.SMEM),
            vec(CONV_CHANNELS),
            vec(ATTN_WIDTH),
            w_o_spec,
        ] + cast_in,
        out_specs=[pl.BlockSpec((1, tm, d), lambda bi, ji: (bi, ji, 0))] + cast_out,
        out_shape=[jax.ShapeDtypeStruct(x.shape, x.dtype)] + cast_shapes,
        scratch_shapes=[
            pltpu.VMEM((SUBLANES, CONV_CHANNELS), F32),
            pltpu.VMEM((N_KV_HEADS, tm // BLOCK, LANES, COLS_PER_KV * BLOCK), BF16),
            pltpu.VMEM((4, tm + BLOCK, LANES), BF16),
            pltpu.VMEM((4, LANES, tm + BLOCK), BF16),
            pltpu.VMEM((n_units, 2 * BLOCK, COLS_PER_KV * BLOCK), F32),
            pltpu.VMEM((n_units, 2 * BLOCK, COLS_PER_KV * BLOCK), BF16),
            pltpu.VMEM((tm, ATTN_WIDTH), F32),
            pltpu.VMEM((tm, CONV_CHANNELS), F32),
            pltpu.VMEM((ATTN_WIDTH, d), BF16),
        ] + own_bf16,
        compiler_params=pltpu.CompilerParams(
            dimension_semantics=("arbitrary", "arbitrary"),
            vmem_limit_bytes=VMEM_LIMIT_BYTES,
        ),
    )(x, g1, w_in, conv_w, kg2, sinks, cog, aog, w_o, *[w for w, _ in to_cast])
    return outs[0], outs[1:]


def _ffn(x2d, g, w_gate, w_up, w_down, layer, tm):
    n, d = x2d.shape
    top2 = lambda i: (0, 0)
    return pl.pallas_call(
        functools.partial(_ffn_kernel, layer=layer),
        name=f"ffn_l{layer}",
        grid=(n // tm,),
        in_specs=[
            pl.BlockSpec((tm, d), lambda i: (i, 0)),
            _resident(g.shape, top2),
            _resident((d, D_FF), top2),
            _resident((d, D_FF), top2),
            _resident((D_FF, d), top2),
        ],
        out_specs=pl.BlockSpec((tm, d), lambda i: (i, 0)),
        out_shape=jax.ShapeDtypeStruct(x2d.shape, x2d.dtype),
        compiler_params=pltpu.CompilerParams(
            dimension_semantics=("arbitrary",),
            vmem_limit_bytes=FFN_VMEM_LIMIT_BYTES,
        ),
    )(x2d, g, w_gate, w_up, w_down)


def kernel(x, norm1_g, w_in, conv_w, q_norm_g, k_norm_g, sinks, conv_out_g, attn_out_g, w_o, norm2_g,
           w_gate, w_up, w_down):
    b, seq, d = x.shape
    depth = w_in.shape[0]
    assert seq % TM_MIXER == 0 and (b * seq) % TM_FFN == 0 and TM_MIXER % BLOCK == 0

    kg2 = jnp.tile(k_norm_g * q_norm_g * (HEAD_DIM ** -0.5 * LOG2E), (1, 2))
    w_in_b, w_o_b = w_in, w_o

    for l in range(depth):
        to_cast = [(w_gate, l), (w_up, l), (w_down, l)]
        if l + 1 < depth:
            to_cast += [(w_in, l + 1), (w_o, l + 1)]
        x, cast = _mixer(x, norm1_g, w_in_b, conv_w, kg2, sinks, conv_out_g, attn_out_g, w_o_b, to_cast, l,
                         TM_MIXER)
        w_gate_b, w_up_b, w_down_b = cast[:3]
        if l + 1 < depth:
            w_in_b, w_o_b = cast[3:]
        x = _ffn(x.reshape(b * seq, d), norm2_g, w_gate_b, w_up_b, w_down_b, l, TM_FFN).reshape(b, seq, d)
    return x
```

```python
import functools

import jax
import jax.numpy as jnp
from jax import lax
from jax.experimental import pallas as pl
from jax.experimental.pallas import tpu as pltpu

D_MODEL = 1024
CONV_CHANNELS = 512
CONV_WIDTH = 3
N_Q_HEADS = 8
N_KV_HEADS = 2
HEAD_DIM = 64
ATTN_WIDTH = N_Q_HEADS * HEAD_DIM
BLOCK = 128
IN_COLS = 3 * CONV_CHANNELS + (N_Q_HEADS + 2 * N_KV_HEADS) * HEAD_DIM
D_FF = 2816
EPS = 1e-6
NEG_INF = -1e30

LANES = 128
SUBLANES = 8
Q_COLS = ATTN_WIDTH // LANES
COLS_PER_KV = Q_COLS // N_KV_HEADS
LOG2E = 1.4426950408889634
KV_OFF = 3 * CONV_CHANNELS
VMEM_LIMIT_BYTES = 60 * 1024 * 1024
SMALL_VMEM_LIMIT_BYTES = 48 * 1024 * 1024

TM_MIXER = 1024
CONV_CHUNK = 256
N_MIXER_IN = 9
TM_FFN = 1024
FF_ROWS = 256
FF_CHUNK = 1024

F32 = jnp.float32
BF16 = jnp.bfloat16


def _rms(x, g):
    ms = jnp.mean(x * x, axis=-1, keepdims=True)
    return x * lax.rsqrt(ms + EPS) * g


def _mixer_kernel(*refs, tm, n_cast, layer, f32_weights):
    (x_ref, g1_ref, win_ref, convw_ref, kg_ref, sink_ref, cog_ref, aog_ref, wo_ref) = refs[:N_MIXER_IN]
    cast_src = refs[N_MIXER_IN:N_MIXER_IN + n_cast]
    out_ref = refs[N_MIXER_IN + n_cast]
    cast_dst = refs[N_MIXER_IN + n_cast + 1:N_MIXER_IN + 2 * n_cast + 1]
    ubuf, qq, kk, vt, sbuf, pbuf, abuf, cbuf, wqt, *own_bf16 = refs[N_MIXER_IN + 2 * n_cast + 1:]
    first_step = (pl.program_id(0) == 0) & (pl.program_id(1) == 0)
    if f32_weights:
        @pl.when(first_step)
        def _():
            for src, dst in zip((win_ref, wo_ref), own_bf16):
                dst[...] = src[...].astype(BF16)
        win_ref, wo_ref = own_bf16
    j = pl.program_id(1)
    this_layer = slice(layer, layer + 1)

    for src, dst in zip(cast_src, cast_dst):
        dst[...] = src[...].astype(BF16)
    c3 = CONV_CHANNELS
    nblk = tm // BLOCK

    @pl.when(first_step)
    def _():
        wqt[...] = win_ref[:, KV_OFF:KV_OFF + ATTN_WIDTH].astype(F32).T.astype(BF16)

    @pl.when(j == 0)
    def _():
        ubuf[0:SUBLANES, :] = jnp.zeros((SUBLANES, c3), F32)
        kk[:, 0:BLOCK, :] = jnp.zeros((4, BLOCK, LANES), BF16)
        vt[:, :, 0:BLOCK] = jnp.zeros((4, LANES, BLOCK), BF16)

    h = _rms(x_ref[0], g1_ref[this_layer, :]).astype(BF16)

    def in_proj(lo, hi):
        return jnp.dot(h, win_ref[:, lo:hi], preferred_element_type=F32)

    def conv_finish(lo, hi, b_gate, c_gate, hc):
        u = c_gate * hc
        w0, w1, w2 = (convw_ref[layer, t:t + 1, lo:hi] for t in range(CONV_WIDTH))
        y = w0 * pltpu.roll(u, 2, 0) + w1 * pltpu.roll(u, 1, 0) + w2 * u
        head = jnp.concatenate([ubuf[:, lo:hi], u[0:SUBLANES, :]], axis=0)
        y_head = (w0 * head[SUBLANES - 2:2 * SUBLANES - 2, :] + w1 * head[SUBLANES - 1:2 * SUBLANES - 1, :]
                  + w2 * u[0:SUBLANES, :])
        ubuf[:, lo:hi] = u[tm - SUBLANES:tm, :]
        cbuf[:, lo:hi] = b_gate * jnp.concatenate([y_head, y[SUBLANES:, :]], axis=0)

    conv_steps = []
    for lo in range(0, c3, CONV_CHUNK):
        parts = []
        for base in (0, c3, 2 * c3):
            conv_steps.append(lambda lo=lo, base=base, parts=parts: parts.append(
                in_proj(base + lo, base + lo + CONV_CHUNK)))
        conv_steps.append(lambda lo=lo, parts=parts: conv_finish(lo, lo + CONV_CHUNK, *parts))
    proj_c = []
    conv_steps.append(lambda: proj_c.append(
        jnp.dot((cbuf[...] * cog_ref[this_layer, :]).astype(BF16), wo_ref[0:c3, :], preferred_element_type=F32)))

    kv = in_proj(KV_OFF + ATTN_WIDTH, IN_COLS)
    q_t = lax.dot_general(wqt[...], h, (((1,), (1,)), ((), ())), preferred_element_type=F32)
    lane = lax.broadcasted_iota(jnp.int32, (1, LANES), 1)
    lo_half = lane < HEAD_DIM

    k_sq = kv[:, 0:LANES] * kv[:, 0:LANES]
    ms_lo = jnp.sum(jnp.where(lo_half, k_sq, 0.0), axis=-1, keepdims=True)
    ms_hi = jnp.sum(jnp.where(lo_half, 0.0, k_sq), axis=-1, keepdims=True)
    kn = kv[:, 0:LANES] * lax.rsqrt(jnp.where(lo_half, ms_lo, ms_hi) * (1.0 / HEAD_DIM) + EPS) * kg_ref[this_layer, :]
    k_a0 = jnp.where(lo_half, kn, 0.0)
    k_b1 = jnp.where(lo_half, 0.0, kn)
    kk[0, BLOCK:BLOCK + tm, :] = k_a0.astype(BF16)
    kk[1, BLOCK:BLOCK + tm, :] = pltpu.roll(k_a0, HEAD_DIM, 1).astype(BF16)
    kk[2, BLOCK:BLOCK + tm, :] = pltpu.roll(k_b1, HEAD_DIM, 1).astype(BF16)
    kk[3, BLOCK:BLOCK + tm, :] = k_b1.astype(BF16)

    for hd in range(N_Q_HEADS):
        c, hf = divmod(hd, 2)
        g, r = divmod(c, COLS_PER_KV)
        t = q_t[hd * HEAD_DIM:(hd + 1) * HEAD_DIM, :]
        inv = lax.rsqrt(jnp.mean(t * t, axis=0, keepdims=True) + EPS)
        for b in range(nblk):
            blk = slice(b * BLOCK, (b + 1) * BLOCK)
            qq[g, b, hf * HEAD_DIM:(hf + 1) * HEAD_DIM, r * BLOCK:(r + 1) * BLOCK] = (
                t[:, blk] * inv[:, blk]).astype(BF16)

    v_t = kv[:, LANES:2 * LANES].T
    ones_row = (lax.broadcasted_iota(jnp.int32, (HEAD_DIM, tm), 0) == 0).astype(F32)
    for g in range(N_KV_HEADS):
        v_g = v_t[g * HEAD_DIM:(g + 1) * HEAD_DIM, :]
        vt[2 * g, :, BLOCK:BLOCK + tm] = jnp.concatenate([v_g, ones_row], axis=0).astype(BF16)
        vt[2 * g + 1, :, BLOCK:BLOCK + tm] = jnp.concatenate([ones_row, v_g], axis=0).astype(BF16)

    pairs = [(b, g) for b in range(nblk) for g in range(N_KV_HEADS)]

    key_i = lax.broadcasted_iota(jnp.int32, (2 * BLOCK, COLS_PER_KV * BLOCK), 0)
    qry_i = lax.broadcasted_iota(jnp.int32, (2 * BLOCK, COLS_PER_KV * BLOCK), 1) & (BLOCK - 1)
    band = (key_i > qry_i) & (key_i <= qry_i + BLOCK)
    first_band = band & (key_i >= jnp.where(j == 0, BLOCK, 0))
    cap = jnp.where(band, jnp.inf, NEG_INF)
    first_col = lax.broadcasted_iota(jnp.int32, (1, COLS_PER_KV * BLOCK), 1) < BLOCK
    first_cap = jnp.where(first_band, jnp.inf, NEG_INF)

    def scores(i):
        b, g = pairs[i]
        for half in range(2):
            keys = kk[2 * g + half, b * BLOCK:(b + 2) * BLOCK, :]
            sbuf[2 * i + half] = jnp.dot(keys, qq[g, b], preferred_element_type=F32)

    sink_term = {}

    def softmax(i):
        b, g = pairs[i]
        for half in range(2):
            s = jnp.minimum(sbuf[2 * i + half], first_cap if b == 0 else cap)
            sink = jnp.where(first_col, sink_ref[layer, 4 * g + half] * LOG2E,
                             sink_ref[layer, 4 * g + 2 + half] * LOG2E)
            m = jnp.maximum(jnp.max(s, axis=0, keepdims=True), sink)
            sink_term[2 * i + half] = jnp.exp2(sink - m)
            pbuf[2 * i + half] = jnp.exp2(s - m).astype(BF16)

    def weighted_values(i):
        b, g = pairs[i]
        keys = slice(b * BLOCK, (b + 2) * BLOCK)
        o_lo = jnp.dot(vt[2 * g, :, keys], pbuf[2 * i], preferred_element_type=F32)
        o_hi = jnp.dot(vt[2 * g + 1, :, keys], pbuf[2 * i + 1], preferred_element_type=F32)
        inv_lo = 1.0 / (o_lo[HEAD_DIM:HEAD_DIM + 1, :] + sink_term[2 * i])
        inv_hi = 1.0 / (o_hi[0:1, :] + sink_term[2 * i + 1])
        o_t = jnp.concatenate([o_lo[0:HEAD_DIM, :] * inv_lo, o_hi[HEAD_DIM:2 * HEAD_DIM, :] * inv_hi], axis=0)
        for r in range(COLS_PER_KV):
            c = COLS_PER_KV * g + r
            abuf[b * BLOCK:(b + 1) * BLOCK, c * LANES:(c + 1) * LANES] = o_t[:, r * BLOCK:(r + 1) * BLOCK].T

    n_iter, n_conv = len(pairs) + 2, len(conv_steps)
    for i in range(n_iter):
        while len(conv_steps) * n_iter > (n_iter - 1 - i) * n_conv:
            conv_steps.pop(0)()
        if i < len(pairs):
            scores(i)
        if 0 <= i - 1 < len(pairs):
            softmax(i - 1)
        if i - 2 >= 0:
            weighted_values(i - 2)

    kk[:, 0:BLOCK, :] = kk[:, tm:tm + BLOCK, :]
    vt[:, :, 0:BLOCK] = vt[:, :, tm:tm + BLOCK]

    mix_a = _rms(abuf[...], aog_ref[this_layer, :]).astype(BF16)
    conv_out = cbuf[...]
    inv_c = lax.rsqrt(jnp.mean(conv_out * conv_out, axis=-1, keepdims=True) + EPS)
    out_ref[0] = (x_ref[0] + proj_c[0] * inv_c
                  + jnp.dot(mix_a, wo_ref[c3:c3 + ATTN_WIDTH, :], preferred_element_type=F32))


def _ffn_kernel(x_ref, g_ref, wg_ref, wu_ref, wd_ref, out_ref, *, layer):
    n_sub = x_ref.shape[0] // FF_ROWS
    chunks = [(lo, min(lo + FF_CHUNK, D_FF)) for lo in range(0, D_FF, FF_CHUNK)]
    items = [(r, c) for r in range(n_sub) for c in range(len(chunks))]

    def rows(r):
        return slice(r * FF_ROWS, (r + 1) * FF_ROWS)

    def normed(r):
        x = x_ref[rows(r), :]
        inv_rms = lax.rsqrt(jnp.mean(x * x, axis=-1, keepdims=True) + EPS)
        return (x * g_ref[layer:layer + 1, :]).astype(BF16), inv_rms

    def gate_up(h, c):
        lo, hi = chunks[c]
        xg, inv_rms = h
        return (jnp.dot(xg, wg_ref[:, lo:hi], preferred_element_type=F32) * inv_rms,
                jnp.dot(xg, wu_ref[:, lo:hi], preferred_element_type=F32) * inv_rms)

    h = {0: normed(0)}
    nxt = gate_up(h[0], 0)
    acc = None
    for k, (r, c) in enumerate(items):
        gate, up = nxt
        if c == 0:
            acc = x_ref[rows(r), :]
            if r + 1 < n_sub:
                h[r + 1] = normed(r + 1)
        if k + 1 < len(items):
            r1, c1 = items[k + 1]
            nxt = gate_up(h[r1], c1)
        act = (gate * jax.nn.sigmoid(gate) * up).astype(BF16)
        lo, hi = chunks[c]
        acc = acc + jnp.dot(act, wd_ref[lo:hi, :], preferred_element_type=F32)
        if c == len(chunks) - 1:
            out_ref[rows(r), :] = acc


def _resident(shape, index_map):
    return pl.BlockSpec(shape, index_map, pipeline_mode=pl.Buffered(1))


def _mixer(x, g1, w_in, conv_w, kg2, sinks, cog, aog, w_o, to_cast, layer, tm):
    b, seq, d = x.shape
    depth = g1.shape[0]
    nj = seq // tm
    steps = b * nj
    lsel3 = lambda bi, ji: (layer, 0, 0)
    top2 = lambda bi, ji: (0, 0)
    vec = lambda n: _resident((depth, n), top2)
    n_units = (tm // BLOCK) * N_KV_HEADS * 2
    cast_in, cast_out, cast_shapes = [], [], []
    for w, wl in to_cast:
        rows, cols = w.shape[1] // steps, w.shape[2]
        assert rows * steps == w.shape[1] and rows % (2 * SUBLANES) == 0
        cast_in.append(pl.BlockSpec((None, rows, cols), lambda bi, ji, wl=wl: (wl, bi * nj + ji, 0)))
        cast_out.append(pl.BlockSpec((rows, cols), lambda bi, ji: (bi * nj + ji, 0)))
        cast_shapes.append(jax.ShapeDtypeStruct(w.shape[1:], BF16))
    f32_weights = w_in.dtype != BF16
    if f32_weights:
        w_in_spec, w_o_spec = _resident((None, d, IN_COLS), lsel3), _resident((None, CONV_CHANNELS + ATTN_WIDTH, d), lsel3)
        own_bf16 = [pltpu.VMEM((d, IN_COLS), BF16), pltpu.VMEM((CONV_CHANNELS + ATTN_WIDTH, d), BF16)]
    else:
        w_in_spec, w_o_spec = _resident((d, IN_COLS), top2), _resident((CONV_CHANNELS + ATTN_WIDTH, d), top2)
        own_bf16 = []
    outs = pl.pallas_call(
        functools.partial(_mixer_kernel, tm=tm, n_cast=len(to_cast), layer=layer, f32_weights=f32_weights),
        name=f"mixer_l{layer}",
        grid=(b, nj),
        in_specs=[
            pl.BlockSpec((1, tm, d), lambda bi, ji: (bi, ji, 0)),
            vec(d),
            w_in_spec,
            _resident(conv_w.shape, lambda bi, ji: (0, 0, 0)),
            vec(LANES),
            pl.BlockSpec(memory_space=pltpu.SMEM),
            vec(CONV_CHANNELS),
            vec(ATTN_WIDTH),
            w_o_spec,
        ] + cast_in,
        out_specs=[pl.BlockSpec((1, tm, d), lambda bi, ji: (bi, ji, 0))] + cast_out,
        out_shape=[jax.ShapeDtypeStruct(x.shape, x.dtype)] + cast_shapes,
        scratch_shapes=[
            pltpu.VMEM((SUBLANES, CONV_CHANNELS), F32),
            pltpu.VMEM((N_KV_HEADS, tm // BLOCK, LANES, COLS_PER_KV * BLOCK), BF16),
            pltpu.VMEM((4, tm + BLOCK, LANES), BF16),
            pltpu.VMEM((4, LANES, tm + BLOCK), BF16),
            pltpu.VMEM((n_units, 2 * BLOCK, COLS_PER_KV * BLOCK), F32),
            pltpu.VMEM((n_units, 2 * BLOCK, COLS_PER_KV * BLOCK), BF16),
            pltpu.VMEM((tm, ATTN_WIDTH), F32),
            pltpu.VMEM((tm, CONV_CHANNELS), F32),
            pltpu.VMEM((ATTN_WIDTH, d), BF16),
        ] + own_bf16,
        compiler_params=pltpu.CompilerParams(
            dimension_semantics=("arbitrary", "arbitrary"),
            vmem_limit_bytes=VMEM_LIMIT_BYTES if f32_weights else SMALL_VMEM_LIMIT_BYTES,
        ),
    )(x, g1, w_in, conv_w, kg2, sinks, cog, aog, w_o, *[w for w, _ in to_cast])
    return outs[0], outs[1:]


def _ffn(x2d, g, w_gate, w_up, w_down, layer, tm):
    n, d = x2d.shape
    top2 = lambda i: (0, 0)
    return pl.pallas_call(
        functools.partial(_ffn_kernel, layer=layer),
        name=f"ffn_l{layer}",
        grid=(n // tm,),
        in_specs=[
            pl.BlockSpec((tm, d), lambda i: (i, 0)),
            _resident(g.shape, top2),
            _resident((d, D_FF), top2),
            _resident((d, D_FF), top2),
            _resident((D_FF, d), top2),
        ],
        out_specs=pl.BlockSpec((tm, d), lambda i: (i, 0)),
        out_shape=jax.ShapeDtypeStruct(x2d.shape, x2d.dtype),
        compiler_params=pltpu.CompilerParams(
            dimension_semantics=("arbitrary",),
            vmem_limit_bytes=SMALL_VMEM_LIMIT_BYTES,
        ),
    )(x2d, g, w_gate, w_up, w_down)


def kernel(x, norm1_g, w_in, conv_w, q_norm_g, k_norm_g, sinks, conv_out_g, attn_out_g, w_o, norm2_g,
           w_gate, w_up, w_down):
    b, seq, d = x.shape
    depth = w_in.shape[0]
    assert seq % TM_MIXER == 0 and (b * seq) % TM_FFN == 0 and TM_MIXER % BLOCK == 0

    kg2 = jnp.tile(k_norm_g * q_norm_g * (HEAD_DIM ** -0.5 * LOG2E), (1, 2))
    w_in_b, w_o_b = w_in, w_o

    for l in range(depth):
        to_cast = [(w_gate, l), (w_up, l), (w_down, l)]
        if l + 1 < depth:
            to_cast += [(w_in, l + 1), (w_o, l + 1)]
        x, cast = _mixer(x, norm1_g, w_in_b, conv_w, kg2, sinks, conv_out_g, attn_out_g, w_o_b, to_cast, l,
                         TM_MIXER)
        w_gate_b, w_up_b, w_down_b = cast[:3]
        if l + 1 < depth:
            w_in_b, w_o_b = cast[3:]
        x = _ffn(x.reshape(b * seq, d), norm2_g, w_gate_b, w_up_b, w_down_b, l, TM_FFN).reshape(b, seq, d)
    return x
```
